```python
import math
import jax, jax.numpy as jnp
from jax import lax
import numpy as np

D_MODEL = 1024
BATCH = 16
SEQ = 2048
DEPTH = 2

N_A = DEPTH // 2
N_B = DEPTH - N_A

H_A = 16
NOPE_A = 64
ROPE_A = 32
V_A = 64
Q_LORA = 768
KV_LORA = 256
WIDTH_A = H_A * V_A
QBLOCK = 128

H_B = 16
HD_B = 64
WIDTH_B = H_B * HD_B
MOBA_BLOCK = 256
MOBA_TOPK = 3
QCHUNK = 8

THETA = 10000.0
LN_EPS = 1e-5
RMS_EPS = 1e-6
ALPHA = (2 * DEPTH) ** 0.25
BETA = (8 * DEPTH) ** (-0.25)

kernel_name = "yoco_mla_moba_gated_deepnorm"


def rope_tables(seq, dim):
    inv = THETA ** (-jnp.arange(0, dim, 2, dtype=jnp.float32) / dim)
    ang = jnp.arange(seq, dtype=jnp.float32)[:, None] * inv[None, :]
    ang = jnp.concatenate([ang, ang], axis=-1)
    return jnp.cos(ang), jnp.sin(ang)


def apply_rope(x, cos, sin):
    x1, x2 = jnp.split(x, 2, axis=-1)
    rot = jnp.concatenate([-x2, x1], axis=-1)
    return (x * cos + rot * sin).astype(x.dtype)


def layer_norm(x, g, b):
    xf = x.astype(jnp.float32)
    mu = jnp.mean(xf, axis=-1, keepdims=True)
    var = jnp.mean(jnp.square(xf - mu), axis=-1, keepdims=True)
    return ((xf - mu) * lax.rsqrt(var + LN_EPS) * g + b).astype(x.dtype)


def rms_norm(x, g):
    xf = x.astype(jnp.float32)
    return (xf * lax.rsqrt(jnp.mean(jnp.square(xf), axis=-1, keepdims=True) + RMS_EPS) * g).astype(x.dtype)


def mla_mixer(x, w_in, q_norm, kv_norm, w_uq, w_ukv, w_o, cos, sin):
    B, S, _ = x.shape
    h = x @ w_in
    c_q, c_kv, k_rope, gate = jnp.split(h, [Q_LORA, Q_LORA + KV_LORA, Q_LORA + KV_LORA + ROPE_A], axis=-1)
    q = (rms_norm(c_q, q_norm) @ w_uq).reshape(B, S, H_A, NOPE_A + ROPE_A)
    q_nope = q[..., :NOPE_A]
    q_rope = apply_rope(q[..., NOPE_A:], cos[None, :, None, :], sin[None, :, None, :])
    kv = (rms_norm(c_kv, kv_norm) @ w_ukv).reshape(B, S, H_A, NOPE_A + V_A)
    k_nope, v = kv[..., :NOPE_A], kv[..., NOPE_A:]
    k_rope = apply_rope(k_rope, cos[None], sin[None])
    scale = (NOPE_A + ROPE_A) ** -0.5
    outs = []
    for i in range(S // QBLOCK):
        q0, kend = i * QBLOCK, (i + 1) * QBLOCK
        s = (jnp.einsum('bqhd,bkhd->bhqk', q_nope[:, q0:kend], k_nope[:, :kend])
             + jnp.einsum('bqhr,bkr->bhqk', q_rope[:, q0:kend], k_rope[:, :kend])).astype(jnp.float32) * scale
        q_pos = q0 + jnp.arange(QBLOCK)
        mask = jnp.arange(kend)[None, :] <= q_pos[:, None]
        p = jax.nn.softmax(jnp.where(mask, s, -jnp.inf), axis=-1).astype(v.dtype)
        outs.append(jnp.einsum('bhqk,bkhd->bqhd', p, v[:, :kend]))
    o = jnp.concatenate(outs, axis=1).reshape(B, S, WIDTH_A)
    return (o * jax.nn.silu(gate)) @ w_o


def moba_shared_kv(x, w_kv, cos, sin):
    B, S, _ = x.shape
    kv = x @ w_kv
    k = kv[..., :WIDTH_B].reshape(B, S, H_B, HD_B)
    v = kv[..., WIDTH_B:].reshape(B, S, H_B, HD_B)
    k = apply_rope(k, cos[None, :, None, :], sin[None, :, None, :])
    nb = -(-S // MOBA_BLOCK)
    pad = nb * MOBA_BLOCK - S
    k = jnp.pad(k, ((0, 0), (0, pad), (0, 0), (0, 0)))
    v = jnp.pad(v, ((0, 0), (0, pad), (0, 0), (0, 0)))
    k_blocks = k.reshape(B, nb, MOBA_BLOCK, H_B, HD_B).transpose(0, 3, 1, 2, 4)
    v_blocks = v.reshape(B, nb, MOBA_BLOCK, H_B, HD_B).transpose(0, 3, 1, 2, 4)
    cnt = jnp.clip(S - jnp.arange(nb) * MOBA_BLOCK, 1, MOBA_BLOCK).astype(jnp.float32)
    k_mean = (jnp.sum(k_blocks.astype(jnp.float32), axis=3) / cnt[:, None]).astype(k.dtype)
    return k_blocks, v_blocks, k_mean


def moba_mixer(x, k_blocks, v_blocks, k_mean, w_in, w_o, cos, sin):
    B, S, _ = x.shape
    nb = k_blocks.shape[2]
    h = x @ w_in
    q, gate = h[..., :WIDTH_B], h[..., WIDTH_B:]
    q = apply_rope(q.reshape(B, S, H_B, HD_B), cos[None, :, None, :], sin[None, :, None, :])
    q = q.transpose(0, 2, 1, 3)
    gs = jnp.einsum('bhsd,bhnd->bhsn', q, k_mean).astype(jnp.float32)
    q_blk = jnp.arange(S) // MOBA_BLOCK
    past = jnp.arange(nb)[None, :] < q_blk[:, None]
    gs = jnp.where(past, gs, -jnp.inf)
    topk = min(MOBA_TOPK, nb)
    _, idx = lax.top_k(gs, topk)
    valid = idx < q_blk[:, None]
    n_chunks = S // QCHUNK
    q_c = q.reshape(B, H_B, n_chunks, QCHUNK, HD_B).transpose(2, 0, 1, 3, 4)
    idx_c = idx.reshape(B, H_B, n_chunks, QCHUNK, topk).transpose(2, 0, 1, 3, 4)
    valid_c = valid.reshape(B, H_B, n_chunks, QCHUNK, topk).transpose(2, 0, 1, 3, 4)
    starts = jnp.arange(n_chunks, dtype=jnp.int32) * QCHUNK
    b_ix = jnp.arange(B)[:, None, None, None]
    h_ix = jnp.arange(H_B)[None, :, None, None]
    scale = HD_B ** -0.5

    def attend(args):
        qq, ii, vv, start = args
        k_sel = k_blocks[b_ix, h_ix, ii]
        v_sel = v_blocks[b_ix, h_ix, ii]
        qb = start // MOBA_BLOCK
        k_own = lax.dynamic_index_in_dim(k_blocks, qb, axis=2, keepdims=False)
        v_own = lax.dynamic_index_in_dim(v_blocks, qb, axis=2, keepdims=False)
        s_sel = jnp.einsum('bhcd,bhctkd->bhctk', qq, k_sel).astype(jnp.float32) * scale
        s_sel = jnp.where(vv[..., None], s_sel, -jnp.inf).reshape(B, H_B, QCHUNK, topk * MOBA_BLOCK)
        s_own = jnp.einsum('bhcd,bhkd->bhck', qq, k_own).astype(jnp.float32) * scale
        k_pos = qb * MOBA_BLOCK + jnp.arange(MOBA_BLOCK)
        q_pos = start + jnp.arange(QCHUNK)
        s_own = jnp.where(k_pos[None, :] <= q_pos[:, None], s_own, -jnp.inf)
        p = jax.nn.softmax(jnp.concatenate([s_sel, s_own], axis=-1), axis=-1).astype(qq.dtype)
        p_sel = p[..., :topk * MOBA_BLOCK].reshape(B, H_B, QCHUNK, topk, MOBA_BLOCK)
        p_own = p[..., topk * MOBA_BLOCK:]
        return (jnp.einsum('bhctk,bhctkd->bhcd', p_sel, v_sel)
                + jnp.einsum('bhck,bhkd->bhcd', p_own, v_own))

    o = lax.map(attend, (q_c, idx_c, valid_c, starts))
    o = o.transpose(1, 0, 3, 2, 4).reshape(B, S, WIDTH_B)
    return (o * jax.nn.silu(gate)) @ w_o


def setup_inputs(seed: int = 0) -> dict:
    key = jax.random.key(seed)
    ks = jax.random.split(key, 14)

    def nrm(k, shape, fan_in, scale=1.0):
        return jax.random.normal(k, shape, jnp.float32) * (scale * fan_in ** -0.5)

    x = jax.random.normal(ks[0], (BATCH, SEQ, D_MODEL), jnp.float32)
    mla_w_in = nrm(ks[1], (N_A, D_MODEL, Q_LORA + KV_LORA + ROPE_A + WIDTH_A), D_MODEL)
    mla_q_norm = 1.0 + 0.02 * jax.random.normal(ks[2], (N_A, Q_LORA), jnp.float32)
    mla_kv_norm = 1.0 + 0.02 * jax.random.normal(ks[3], (N_A, KV_LORA), jnp.float32)
    mla_w_uq = nrm(ks[4], (N_A, Q_LORA, H_A * (NOPE_A + ROPE_A)), Q_LORA)
    mla_w_ukv = nrm(ks[5], (N_A, KV_LORA, H_A * (NOPE_A + V_A)), KV_LORA)
    mla_w_o = nrm(ks[6], (N_A, WIDTH_A, D_MODEL), WIDTH_A, BETA)
    moba_w_kv = nrm(ks[7], (D_MODEL, 2 * WIDTH_B), D_MODEL)
    moba_w_in = nrm(ks[8], (N_B, D_MODEL, 2 * WIDTH_B), D_MODEL)
    moba_w_o = nrm(ks[9], (N_B, WIDTH_B, D_MODEL), WIDTH_B, BETA)
    ln_g = 1.0 + 0.02 * jax.random.normal(ks[10], (DEPTH, D_MODEL), jnp.float32)
    ln_b = 0.02 * jax.random.normal(ks[11], (DEPTH, D_MODEL), jnp.float32)
    return {"x": x, "mla_w_in": mla_w_in, "mla_q_norm": mla_q_norm, "mla_kv_norm": mla_kv_norm,
            "mla_w_uq": mla_w_uq, "mla_w_ukv": mla_w_ukv, "mla_w_o": mla_w_o,
            "moba_w_kv": moba_w_kv, "moba_w_in": moba_w_in, "moba_w_o": moba_w_o,
            "ln_g": ln_g, "ln_b": ln_b}


def reference(x, mla_w_in, mla_q_norm, mla_kv_norm, mla_w_uq, mla_w_ukv, mla_w_o,
              moba_w_kv, moba_w_in, moba_w_o, ln_g, ln_b):
    S = x.shape[1]
    cos_a, sin_a = rope_tables(S, ROPE_A)
    cos_b, sin_b = rope_tables(S, HD_B)
    shared = None
    for layer in range(DEPTH):
        if layer < N_A:
            y = mla_mixer(x, mla_w_in[layer], mla_q_norm[layer], mla_kv_norm[layer],
                          mla_w_uq[layer], mla_w_ukv[layer], mla_w_o[layer], cos_a, sin_a)
        else:
            if shared is None:
                shared = moba_shared_kv(x, moba_w_kv, cos_b, sin_b)
            j = layer - N_A
            y = moba_mixer(x, shared[0], shared[1], shared[2], moba_w_in[j], moba_w_o[j], cos_b, sin_b)
        x = layer_norm(ALPHA * x + y, ln_g[layer], ln_b[layer])
    return x
```

```python
import functools

import jax
import jax.numpy as jnp
from jax import lax
from jax.experimental import pallas as pl
from jax.experimental.pallas import tpu as pltpu

D_MODEL = 1024
H = 16
NOPE = 64
ROPE = 32
V_HD = 64
Q_LORA = 768
KV_LORA = 256
HD_B = 64
MOBA_BLOCK = 256
MOBA_TOPK = 3
THETA = 10000.0
LN_EPS = 1e-5
RMS_EPS = 1e-6
DEPTH = 2
ALPHA = (2 * DEPTH) ** 0.25

TM = MOBA_BLOCK
TM_OUT = 512
SLAB = 128
PAIR = 2 * SLAB
NEG = -1e30
VMEM_LIMIT = 48 * 1024 * 1024

F32 = jnp.float32
BF16 = jnp.bfloat16
_NT = (((1,), (1,)), ((), ()))


def _dot(a, b):
    return jnp.dot(a, b, preferred_element_type=F32)


def _dot_nt(a, b):
    return lax.dot_general(a, b, _NT, preferred_element_type=F32)


def _rms_norm(c, gain):
    return c * lax.rsqrt(jnp.mean(c * c, axis=-1, keepdims=True) + RMS_EPS) * gain


def _silu(g):
    return g * jax.nn.sigmoid(g)


def _mla_proj_kernel(x_ref, wa_ref, qn_ref, kvn_ref, wqT_ref, wkn_ref, wvT_ref, wgT_ref, tq_ref, tkr_ref,
                     qT_ref, kn_ref, kr_ref, vT_ref, gT_ref):
    xb = x_ref[0].astype(BF16)
    h = _dot(xb, wa_ref[...])
    cq_n = _rms_norm(h[:, :Q_LORA], qn_ref[...]).astype(BF16)
    ckv_n = _rms_norm(h[:, Q_LORA:Q_LORA + KV_LORA], kvn_ref[...]).astype(BF16)
    qT = _dot_nt(wqT_ref[...], cq_n)
    qT = qT.reshape(H // 2, PAIR, TM) * tq_ref[...][None]
    qT_ref[0] = qT.reshape(H * SLAB, TM).astype(BF16)
    kn_ref[0] = _dot(ckv_n, wkn_ref[...]).astype(BF16)
    vT_ref[0] = _dot_nt(wvT_ref[...], ckv_n).astype(BF16)
    t = h[:, Q_LORA + KV_LORA:] * tkr_ref[...]
    kr_ref[0] = (t + pltpu.roll(t, ROPE, 1)).astype(BF16)
    gT_ref[0] = _silu(_dot_nt(wgT_ref[...], xb)).astype(BF16)


def _mla_proj(x, wa, qn, kvn, wqT, wkn, wvT, wgT, tq, tkr):
    B, S, D = x.shape
    n = S // TM
    wa_n = wa.shape[1]
    full = lambda a: pl.BlockSpec(a.shape, lambda b, i: (0,) * a.ndim)
    return pl.pallas_call(
        _mla_proj_kernel,
        grid=(B, n),
        in_specs=[
            pl.BlockSpec((1, TM, D), lambda b, i: (b, i, 0)),
            full(wa), full(qn), full(kvn), full(wqT), full(wkn), full(wvT), full(wgT),
            pl.BlockSpec((PAIR, TM), lambda b, i: (0, i)),
            pl.BlockSpec((TM, 128), lambda b, i: (i, 0)),
        ],
        out_specs=[
            pl.BlockSpec((1, H * SLAB, TM), lambda b, i: (b, 0, i)),
            pl.BlockSpec((1, TM, H * NOPE), lambda b, i: (b, i, 0)),
            pl.BlockSpec((1, TM, 128), lambda b, i: (b, i, 0)),
            pl.BlockSpec((1, H * V_HD, TM), lambda b, i: (b, 0, i)),
            pl.BlockSpec((1, H * V_HD, TM), lambda b, i: (b, 0, i)),
        ],
        out_shape=[
            jax.ShapeDtypeStruct((B, H * SLAB, S), BF16),
            jax.ShapeDtypeStruct((B, S, H * NOPE), BF16),
            jax.ShapeDtypeStruct((B, S, 128), BF16),
            jax.ShapeDtypeStruct((B, H * V_HD, S), BF16),
            jax.ShapeDtypeStruct((B, H * V_HD, S), BF16),
        ],
        compiler_params=pltpu.CompilerParams(
            dimension_semantics=("arbitrary", "arbitrary"), vmem_limit_bytes=VMEM_LIMIT),
        name="mla_proj",
    )(x, wa, qn, kvn, wqT, wkn, wvT, wgT, tq, tkr)


def _causal_mask():
    r = lax.broadcasted_iota(jnp.int32, (TM, TM), 0)
    c = lax.broadcasted_iota(jnp.int32, (TM, TM), 1)
    return r <= c


def _attend(q_slab, k_rows, vT_rows, n_past, bias_rows):
    s = _dot(k_rows, q_slab)
    s_diag = jnp.where(_causal_mask(), s[n_past * TM:], NEG)
    m = jnp.max(s_diag, axis=0, keepdims=True)
    blocks = [s[j * TM:(j + 1) * TM] for j in range(n_past)]
    for j, sj in enumerate(blocks):
        cm = jnp.max(sj, axis=0, keepdims=True)
        if bias_rows is not None:
            cm = cm + bias_rows[j]
        m = jnp.maximum(m, cm)
    p_diag = jnp.exp(s_diag - m)
    l = jnp.sum(p_diag, axis=0, keepdims=True)
    ps = []
    for j, sj in enumerate(blocks):
        shift = m if bias_rows is None else m - bias_rows[j]
        pj = jnp.exp(sj - shift)
        l = l + jnp.sum(pj, axis=0, keepdims=True)
        ps.append(pj.astype(BF16))
    ps.append(p_diag.astype(BF16))
    p = ps[0] if len(ps) == 1 else jnp.concatenate(ps, axis=0)
    return _dot(vT_rows, p) / l


def _mla_attn_kernel(qT_ref, kn_ref, kr_ref, vT_ref, gT_ref, og_ref, kslab_ref):
    S = kn_ref.shape[1]
    lane = lax.broadcasted_iota(jnp.int32, (S, SLAB), 1)
    kn = kn_ref[0]
    kr = kr_ref[0]
    kslab_ref[0] = jnp.where(lane < NOPE, kn, kr)
    kslab_ref[1] = jnp.where(lane >= NOPE, kn, kr)
    for qi in range(S // TM):
        cols = slice(qi * TM, (qi + 1) * TM)
        keys = (qi + 1) * TM
        outs = []
        for hh in range(2):
            outs.append(_attend(qT_ref[0, hh * SLAB:(hh + 1) * SLAB, cols],
                                kslab_ref[hh, :keys, :],
                                vT_ref[0, hh * V_HD:(hh + 1) * V_HD, :keys],
                                qi, None))
        oT = jnp.concatenate(outs, axis=0) * gT_ref[0, :, cols].astype(F32)
        og_ref[0, cols, :] = oT.T.astype(BF16)


def _mla_attn(qT, kn, kr, vT, gT):
    B, _, S = qT.shape
    return pl.pallas_call(
        _mla_attn_kernel,
        grid=(B, H // 2),
        in_specs=[
            pl.BlockSpec((1, PAIR, S), lambda b, p: (b, p, 0)),
            pl.BlockSpec((1, S, 2 * NOPE), lambda b, p: (b, 0, p)),
            pl.BlockSpec((1, S, 128), lambda b, p: (b, 0, 0)),
            pl.BlockSpec((1, 2 * V_HD, S), lambda b, p: (b, p, 0)),
            pl.BlockSpec((1, 2 * V_HD, S), lambda b, p: (b, p, 0)),
        ],
        out_specs=pl.BlockSpec((1, S, 2 * V_HD), lambda b, p: (b, 0, p)),
        out_shape=jax.ShapeDtypeStruct((B, S, H * V_HD), BF16),
        scratch_shapes=[pltpu.VMEM((2, S, SLAB), BF16)],
        compiler_params=pltpu.CompilerParams(
            dimension_semantics=("arbitrary", "arbitrary"), vmem_limit_bytes=VMEM_LIMIT),
        name="mla_attn",
    )(qT, kn, kr, vT, gT)


def _out_ln_kernel(og_ref, x_ref, wo_ref, g_ref, b_ref, o_ref):
    z = ALPHA * x_ref[...] + _dot(og_ref[...], wo_ref[...])
    mu = jnp.mean(z, axis=-1, keepdims=True)
    zc = z - mu
    var = jnp.mean(zc * zc, axis=-1, keepdims=True)
    o_ref[...] = zc * lax.rsqrt(var + LN_EPS) * g_ref[...] + b_ref[...]


def _out_ln(og, x, wo, g, b):
    R, D = x.shape
    W = og.shape[1]
    return pl.pallas_call(
        _out_ln_kernel,
        grid=(R // TM_OUT,),
        in_specs=[
            pl.BlockSpec((TM_OUT, W), lambda i: (i, 0)),
            pl.BlockSpec((TM_OUT, D), lambda i: (i, 0)),
            pl.BlockSpec((W, D), lambda i: (0, 0)),
            pl.BlockSpec((1, D), lambda i: (0, 0)),
            pl.BlockSpec((1, D), lambda i: (0, 0)),
        ],
        out_specs=pl.BlockSpec((TM_OUT, D), lambda i: (i, 0)),
        out_shape=jax.ShapeDtypeStruct((R, D), F32),
        compiler_params=pltpu.CompilerParams(
            dimension_semantics=("arbitrary",), vmem_limit_bytes=VMEM_LIMIT),
        name="out_ln",
    )(og, x, wo, g, b)


def _moba_proj_kernel(x_ref, wk_ref, wvT_ref, wqT_ref, wgT_ref, ck_ref, s1_ref, s2_ref, cT_ref, sT_ref,
                      k_ref, vT_ref, qT_ref, gT_ref, sb_ref, kmt_ref):
    i = pl.program_id(1)
    nb = kmt_ref.shape[0] // H

    @pl.when(i == 0)
    def _():
        kmt_ref[...] = jnp.zeros_like(kmt_ref)

    xb = x_ref[0].astype(BF16)
    k = _dot(xb, wk_ref[...])
    parts = []
    for c in range(H * HD_B // 128):
        kc = k[:, c * 128:(c + 1) * 128]
        parts.append(kc * ck_ref[...] + pltpu.roll(kc, 128 - HD_B // 2, 1) * s1_ref[...]
                     + pltpu.roll(kc, HD_B // 2, 1) * s2_ref[...])
    k_r = jnp.concatenate(parts, axis=1)
    k_ref[0] = k_r.astype(BF16)
    vT_ref[0] = _dot_nt(wvT_ref[...], xb).astype(BF16)
    gT_ref[0] = _silu(_dot_nt(wgT_ref[...], xb)).astype(BF16)

    qT = _dot_nt(wqT_ref[...], xb).reshape(H, HD_B, TM)
    rot = jnp.concatenate([-qT[:, HD_B // 2:], qT[:, :HD_B // 2]], axis=1)
    q_r = (qT * cT_ref[...][None] + rot * sT_ref[...][None]).reshape(H * HD_B, TM).astype(BF16)
    qT_ref[0] = q_r

    g = _dot(kmt_ref[...].astype(BF16), q_r).reshape(nb, H, TM)
    g_eff = [g[j] + jnp.where(j < i, 0.0, NEG) for j in range(nb)]
    rows = []
    for j in range(nb):
        rank = jnp.zeros((H, TM), F32)
        for j2 in range(nb):
            if j2 == j:
                continue
            beats = (g_eff[j2] >= g_eff[j]) if j2 < j else (g_eff[j2] > g_eff[j])
            rank = rank + jnp.where(beats, 1.0, 0.0)
        rows.append(jnp.where(rank < float(MOBA_TOPK), jnp.where(j < i, 0.0, NEG), NEG))
    sb_ref[0] = jnp.concatenate(rows, axis=0)

    k_mean = jnp.sum(k_r, axis=0, keepdims=True) * (1.0 / TM)
    head_of_lane = lax.broadcasted_iota(jnp.int32, (H, H * HD_B), 1) // HD_B
    head_of_row = lax.broadcasted_iota(jnp.int32, (H, H * HD_B), 0)
    kmt_ref[pl.ds(pl.multiple_of(i * H, H), H), :] = jnp.where(
        head_of_lane == head_of_row, jnp.broadcast_to(k_mean, (H, H * HD_B)), 0.0)


def _moba_proj(x, wk, wvT, wqT, wgT, ck, s1, s2, cT, sT):
    B, S, D = x.shape
    n = S // TM
    W = H * HD_B
    full = lambda a: pl.BlockSpec(a.shape, lambda b, i: (0,) * a.ndim)
    return pl.pallas_call(
        _moba_proj_kernel,
        grid=(B, n),
        in_specs=[
            pl.BlockSpec((1, TM, D), lambda b, i: (b, i, 0)),
            full(wk), full(wvT), full(wqT), full(wgT),
            pl.BlockSpec((TM, 128), lambda b, i: (i, 0)),
            pl.BlockSpec((TM, 128), lambda b, i: (i, 0)),
            pl.BlockSpec((TM, 128), lambda b, i: (i, 0)),
            pl.BlockSpec((HD_B, TM), lambda b, i: (0, i)),
            pl.BlockSpec((HD_B, TM), lambda b, i: (0, i)),
        ],
        out_specs=[
            pl.BlockSpec((1, TM, W), lambda b, i: (b, i, 0)),
            pl.BlockSpec((1, W, TM), lambda b, i: (b, 0, i)),
            pl.BlockSpec((1, W, TM), lambda b, i: (b, 0, i)),
            pl.BlockSpec((1, W, TM), lambda b, i: (b, 0, i)),
            pl.BlockSpec((1, n * H, TM), lambda b, i: (b, 0, i)),
        ],
        out_shape=[
            jax.ShapeDtypeStruct((B, S, W), BF16),
            jax.ShapeDtypeStruct((B, W, S), BF16),
            jax.ShapeDtypeStruct((B, W, S), BF16),
            jax.ShapeDtypeStruct((B, W, S), BF16),
            jax.ShapeDtypeStruct((B, n * H, S), F32),
        ],
        scratch_shapes=[pltpu.VMEM((n * H, W), F32)],
        compiler_params=pltpu.CompilerParams(
            dimension_semantics=("arbitrary", "arbitrary"), vmem_limit_bytes=VMEM_LIMIT),
        name="moba_proj",
    )(x, wk, wvT, wqT, wgT, ck, s1, s2, cT, sT)


def _moba_attn_kernel(qT_ref, k_ref, vT_ref, gT_ref, sb_ref, og_ref):
    S = k_ref.shape[1]
    pair = pl.program_id(1)
    zeros = jnp.zeros((HD_B, TM), BF16)
    for qi in range(S // TM):
        cols = slice(qi * TM, (qi + 1) * TM)
        keys = (qi + 1) * TM
        outs = []
        for hh in range(2):
            q = qT_ref[0, hh * HD_B:(hh + 1) * HD_B, cols]
            q_slab = jnp.concatenate([q, zeros] if hh == 0 else [zeros, q], axis=0)
            bias = [sb_ref[0, pl.ds(j * H + 2 * pair + hh, 1), cols] for j in range(qi)]
            outs.append(_attend(q_slab, k_ref[0, :keys, :],
                                vT_ref[0, hh * HD_B:(hh + 1) * HD_B, :keys], qi, bias))
        oT = jnp.concatenate(outs, axis=0) * gT_ref[0, :, cols].astype(F32)
        og_ref[0, cols, :] = oT.T.astype(BF16)


def _moba_attn(qT, k, vT, gT, sb):
    B, W, S = qT.shape
    return pl.pallas_call(
        _moba_attn_kernel,
        grid=(B, H // 2),
        in_specs=[
            pl.BlockSpec((1, 2 * HD_B, S), lambda b, p: (b, p, 0)),
            pl.BlockSpec((1, S, 2 * HD_B), lambda b, p: (b, 0, p)),
            pl.BlockSpec((1, 2 * HD_B, S), lambda b, p: (b, p, 0)),
            pl.BlockSpec((1, 2 * HD_B, S), lambda b, p: (b, p, 0)),
            pl.BlockSpec((1, sb.shape[1], S), lambda b, p: (b, 0, 0)),
        ],
        out_specs=pl.BlockSpec((1, S, 2 * HD_B), lambda b, p: (b, 0, p)),
        out_shape=jax.ShapeDtypeStruct((B, S, W), BF16),
        compiler_params=pltpu.CompilerParams(
            dimension_semantics=("arbitrary", "arbitrary"), vmem_limit_bytes=VMEM_LIMIT),
        name="moba_attn",
    )(qT, k, vT, gT, sb)


def _rope_tables(seq, dim):
    inv = THETA ** (-jnp.arange(0, dim, 2, dtype=F32) / dim)
    ang = jnp.arange(seq, dtype=F32)[:, None] * inv[None, :]
    ang = jnp.concatenate([ang, ang], axis=-1)
    return jnp.cos(ang), jnp.sin(ang)


def _rot_cols(w):
    half = w.shape[-1] // 2
    return jnp.concatenate([-w[..., half:], w[..., :half]], axis=-1)


def _mla_layer(x, w_in, q_norm, kv_norm, w_uq, w_ukv, w_o, ln_g, ln_b, cos, sin):
    B, S, D = x.shape
    scale = (NOPE + ROPE) ** -0.5
    w_cq = w_in[:, :Q_LORA]
    w_ckv = w_in[:, Q_LORA:Q_LORA + KV_LORA]
    w_kr = w_in[:, Q_LORA + KV_LORA:Q_LORA + KV_LORA + ROPE]
    w_g = w_in[:, Q_LORA + KV_LORA + ROPE:]
    w_kr_rot = _rot_cols(w_kr)
    wa = jnp.concatenate([w_cq, w_ckv, w_kr, w_kr_rot, w_kr, w_kr_rot], axis=1).astype(BF16)

    wq = w_uq.reshape(Q_LORA, H, NOPE + ROPE)
    wq_n, wq_r = wq[..., :NOPE], wq[..., NOPE:]
    wq_rr = _rot_cols(wq_r)
    even = jnp.concatenate([wq_n, wq_r, wq_rr], axis=-1)
    odd = jnp.concatenate([wq_r, wq_rr, wq_n], axis=-1)
    is_even = (jnp.arange(H) % 2 == 0)[None, :, None]
    wqT = jnp.where(is_even, even, odd).reshape(Q_LORA, H * SLAB).T.astype(BF16)

    wkv = w_ukv.reshape(KV_LORA, H, NOPE + V_HD)
    wkn = wkv[..., :NOPE].reshape(KV_LORA, H * NOPE).astype(BF16)
    wvT = wkv[..., NOPE:].reshape(KV_LORA, H * V_HD).T.astype(BF16)
    wgT = w_g.T.astype(BF16)

    cT, sT = cos.T * scale, sin.T * scale
    ones = jnp.full((NOPE, S), scale, F32)
    tq = jnp.concatenate([ones, cT, sT, cT, sT, ones], axis=0)
    tkr = jnp.concatenate([cos, sin, cos, sin], axis=1)

    qT, kn, kr, vT, gT = _mla_proj(x, wa, q_norm[None], kv_norm[None], wqT, wkn, wvT, wgT, tq, tkr)
    og = _mla_attn(qT, kn, kr, vT, gT)
    out = _out_ln(og.reshape(B * S, H * V_HD), x.reshape(B * S, D), w_o.astype(BF16), ln_g[None], ln_b[None])
    return out.reshape(B, S, D)


def _moba_layer(x, w_kv, w_in, w_o, ln_g, ln_b, cos, sin):
    B, S, D = x.shape
    W = H * HD_B
    scale = HD_B ** -0.5
    half = HD_B // 2
    wk = w_kv[:, :W].astype(BF16)
    wvT = w_kv[:, W:].T.astype(BF16)
    wqT = w_in[:, :W].T.astype(BF16)
    wgT = w_in[:, W:].T.astype(BF16)
    zeros = jnp.zeros((S, half), F32)
    ck = jnp.concatenate([cos, cos], axis=1)
    s1 = jnp.tile(jnp.concatenate([-sin[:, :half], zeros], axis=1), (1, 2))
    s2 = jnp.tile(jnp.concatenate([zeros, sin[:, half:]], axis=1), (1, 2))
    cT, sT = cos.T * scale, sin.T * scale
    k, vT, qT, gT, sb = _moba_proj(x, wk, wvT, wqT, wgT, ck, s1, s2, cT, sT)
    og = _moba_attn(qT, k, vT, gT, sb)
    out = _out_ln(og.reshape(B * S, W), x.reshape(B * S, D), w_o.astype(BF16), ln_g[None], ln_b[None])
    return out.reshape(B, S, D)


def kernel(x, mla_w_in, mla_q_norm, mla_kv_norm, mla_w_uq, mla_w_ukv, mla_w_o,
           moba_w_kv, moba_w_in, moba_w_o, ln_g, ln_b):
    S = x.shape[1]
    n_a = mla_w_in.shape[0]
    n_b = moba_w_in.shape[0]
    cos_a, sin_a = _rope_tables(S, ROPE)
    cos_b, sin_b = _rope_tables(S, HD_B)
    for l in range(n_a):
        x = _mla_layer(x, mla_w_in[l], mla_q_norm[l], mla_kv_norm[l], mla_w_uq[l], mla_w_ukv[l], mla_w_o[l],
                       ln_g[l], ln_b[l], cos_a, sin_a)
    for j in range(n_b):
        assert n_b == 1
        x = _moba_layer(x, moba_w_kv, moba_w_in[j], moba_w_o[j], ln_g[n_a + j], ln_b[n_a + j], cos_b, sin_b)
    return x
```

```python
import jax
import jax.numpy as jnp
from jax import lax
from jax.experimental import pallas as pl
from jax.experimental.pallas import tpu as pltpu

D_MODEL = 1024
H = 16
NOPE = 64
ROPE = 32
V_HD = 64
Q_LORA = 768
KV_LORA = 256
HD_B = 64
MOBA_BLOCK = 256
MOBA_TOPK = 3
THETA = 10000.0
LN_EPS = 1e-5
RMS_EPS = 1e-6
DEPTH = 2
ALPHA = (2 * DEPTH) ** 0.25

TM = MOBA_BLOCK
TM_OUT = 512
LANES = 128
BF16_ROWS = 16
SLAB = LANES
PAIR = 2 * SLAB
LOG2E = 1.4426950408889634
NEG = -1e30
VMEM_LIMIT = 48 * 1024 * 1024

F32 = jnp.float32
BF16 = jnp.bfloat16
_NT = (((1,), (1,)), ((), ()))


def _dot(a, b):
    return jnp.dot(a, b, preferred_element_type=F32)


def _dot_nt(a, b):
    return lax.dot_general(a, b, _NT, preferred_element_type=F32)


def _rms_norm(c, gain):
    return c * lax.rsqrt(jnp.mean(c * c, axis=-1, keepdims=True) + RMS_EPS) * gain


def _silu(g):
    return g * jax.nn.sigmoid(g)


def _mla_proj_kernel(x_ref, wa_ref, qn_ref, kvn_ref, wqT_ref, wkn_ref, wvT_ref, wgT_ref, tq_ref, tkr_ref,
                     qT_ref, kn_ref, kr_ref, vT_ref, gT_ref):
    xb = x_ref[0].astype(BF16)
    h = _dot(xb, wa_ref[...])
    cq_n = _rms_norm(h[:, :Q_LORA], qn_ref[...]).astype(BF16)
    ckv_n = _rms_norm(h[:, Q_LORA:Q_LORA + KV_LORA], kvn_ref[...]).astype(BF16)
    qT = _dot_nt(wqT_ref[...], cq_n)
    qT = qT.reshape(H // 2, PAIR, TM) * tq_ref[...][None]
    qT_ref[0] = qT.reshape(H * SLAB, TM).astype(BF16)
    kn_ref[0] = _dot(ckv_n, wkn_ref[...]).astype(BF16)
    vT_ref[0] = _dot_nt(wvT_ref[...], ckv_n).astype(BF16)
    t = h[:, Q_LORA + KV_LORA:] * tkr_ref[...]
    kr_ref[0] = (t + pltpu.roll(t, ROPE, 1)).astype(BF16)
    gT_ref[0] = _silu(_dot_nt(wgT_ref[...], xb)).astype(BF16)


def _mla_proj(x, wa, qn, kvn, wqT, wkn, wvT, wgT, tq, tkr):
    B, S, D = x.shape
    n = S // TM
    full = lambda a: pl.BlockSpec(a.shape, lambda b, i: (0,) * a.ndim)
    return pl.pallas_call(
        _mla_proj_kernel,
        grid=(B, n),
        in_specs=[
            pl.BlockSpec((1, TM, D), lambda b, i: (b, i, 0)),
            full(wa), full(qn), full(kvn), full(wqT), full(wkn), full(wvT), full(wgT),
            pl.BlockSpec((PAIR, TM), lambda b, i: (0, i)),
            pl.BlockSpec((TM, LANES), lambda b, i: (i, 0)),
        ],
        out_specs=[
            pl.BlockSpec((1, H * SLAB, TM), lambda b, i: (b, 0, i)),
            pl.BlockSpec((1, TM, H * NOPE), lambda b, i: (b, i, 0)),
            pl.BlockSpec((1, TM, LANES), lambda b, i: (b, i, 0)),
            pl.BlockSpec((1, H * V_HD, TM), lambda b, i: (b, 0, i)),
            pl.BlockSpec((1, H * V_HD, TM), lambda b, i: (b, 0, i)),
        ],
        out_shape=[
            jax.ShapeDtypeStruct((B, H * SLAB, S), BF16),
            jax.ShapeDtypeStruct((B, S, H * NOPE), BF16),
            jax.ShapeDtypeStruct((B, S, LANES), BF16),
            jax.ShapeDtypeStruct((B, H * V_HD, S), BF16),
            jax.ShapeDtypeStruct((B, H * V_HD, S), BF16),
        ],
        compiler_params=pltpu.CompilerParams(
            dimension_semantics=("arbitrary", "arbitrary"), vmem_limit_bytes=VMEM_LIMIT),
        name="mla_proj",
    )(x, wa, qn, kvn, wqT, wkn, wvT, wgT, tq, tkr)


def _causal_mask():
    r = lax.broadcasted_iota(jnp.int32, (TM, TM), 0)
    c = lax.broadcasted_iota(jnp.int32, (TM, TM), 1)
    return r <= c


def _softmax_weights(s, n_past, bias_rows):
    s_diag = jnp.where(_causal_mask(), s[n_past * TM:], NEG)
    m = jnp.max(s_diag, axis=0, keepdims=True)
    blocks = [s[j * TM:(j + 1) * TM] for j in range(n_past)]
    for j, sj in enumerate(blocks):
        cm = jnp.max(sj, axis=0, keepdims=True)
        if bias_rows is not None:
            cm = cm + bias_rows[j]
        m = jnp.maximum(m, cm)
    ps = []
    for j, sj in enumerate(blocks):
        shift = m if bias_rows is None else m - bias_rows[j]
        ps.append(jnp.exp2(sj - shift).astype(BF16))
    ps.append(jnp.exp2(s_diag - m).astype(BF16))
    return ps[0] if len(ps) == 1 else jnp.concatenate(ps, axis=0)


def _weighted_values(vT_rows, p):
    hd = vT_rows.shape[0]
    ones = jnp.ones((BF16_ROWS, vT_rows.shape[1]), BF16)
    o = _dot(jnp.concatenate([vT_rows, ones], axis=0), p)
    return o[:hd] / o[hd:hd + 1]


def _attend_pair(n_q, scores, weights, values, gT_ref, og_ref):
    s_next = [scores(0, hh) for hh in range(2)]
    for qi in range(n_q):
        s_cur = s_next
        if qi + 1 < n_q:
            s_next = [scores(qi + 1, hh) for hh in range(2)]
        p = [weights(qi, hh, s_cur[hh]) for hh in range(2)]
        outs = [values(qi, hh, p[hh]) for hh in range(2)]
        cols = slice(qi * TM, (qi + 1) * TM)
        oT = jnp.concatenate(outs, axis=0) * gT_ref[0, :, cols].astype(F32)
        og_ref[0, cols, :] = oT.T.astype(BF16)


def _mla_attn_kernel(qT_ref, kn_ref, kr_ref, vT_ref, gT_ref, og_ref, kslab_ref):
    S = kn_ref.shape[1]
    lane = lax.broadcasted_iota(jnp.int32, (S, SLAB), 1)
    kn = kn_ref[0]
    kr = kr_ref[0]
    kslab_ref[0] = jnp.where(lane < NOPE, kn, kr)
    kslab_ref[1] = jnp.where(lane >= NOPE, kn, kr)

    def scores(qi, hh):
        q_slab = qT_ref[0, hh * SLAB:(hh + 1) * SLAB, qi * TM:(qi + 1) * TM]
        return _dot(kslab_ref[hh, :(qi + 1) * TM, :], q_slab)

    def weights(qi, hh, s):
        return _softmax_weights(s, qi, None)

    def values(qi, hh, p):
        return _weighted_values(vT_ref[0, hh * V_HD:(hh + 1) * V_HD, :(qi + 1) * TM], p)

    _attend_pair(S // TM, scores, weights, values, gT_ref, og_ref)


def _mla_attn(qT, kn, kr, vT, gT):
    B, _, S = qT.shape
    return pl.pallas_call(
        _mla_attn_kernel,
        grid=(B, H // 2),
        in_specs=[
            pl.BlockSpec((1, PAIR, S), lambda b, p: (b, p, 0)),
            pl.BlockSpec((1, S, 2 * NOPE), lambda b, p: (b, 0, p)),
            pl.BlockSpec((1, S, LANES), lambda b, p: (b, 0, 0)),
            pl.BlockSpec((1, 2 * V_HD, S), lambda b, p: (b, p, 0)),
            pl.BlockSpec((1, 2 * V_HD, S), lambda b, p: (b, p, 0)),
        ],
        out_specs=pl.BlockSpec((1, S, 2 * V_HD), lambda b, p: (b, 0, p)),
        out_shape=jax.ShapeDtypeStruct((B, S, H * V_HD), BF16),
        scratch_shapes=[pltpu.VMEM((2, S, SLAB), BF16)],
        compiler_params=pltpu.CompilerParams(
            dimension_semantics=("arbitrary", "arbitrary"), vmem_limit_bytes=VMEM_LIMIT),
        name="mla_attn",
    )(qT, kn, kr, vT, gT)


def _out_ln_kernel(og_ref, x_ref, wo_ref, g_ref, b_ref, o_ref):
    z = ALPHA * x_ref[...] + _dot(og_ref[...], wo_ref[...])
    mu = jnp.mean(z, axis=-1, keepdims=True)
    zc = z - mu
    var = jnp.mean(zc * zc, axis=-1, keepdims=True)
    o_ref[...] = zc * lax.rsqrt(var + LN_EPS) * g_ref[...] + b_ref[...]


def _out_ln(og, x, wo, g, b):
    R, D = x.shape
    W = og.shape[1]
    return pl.pallas_call(
        _out_ln_kernel,
        grid=(R // TM_OUT,),
        in_specs=[
            pl.BlockSpec((TM_OUT, W), lambda i: (i, 0)),
            pl.BlockSpec((TM_OUT, D), lambda i: (i, 0)),
            pl.BlockSpec((W, D), lambda i: (0, 0)),
            pl.BlockSpec((1, D), lambda i: (0, 0)),
            pl.BlockSpec((1, D), lambda i: (0, 0)),
        ],
        out_specs=pl.BlockSpec((TM_OUT, D), lambda i: (i, 0)),
        out_shape=jax.ShapeDtypeStruct((R, D), F32),
        compiler_params=pltpu.CompilerParams(
            dimension_semantics=("arbitrary",), vmem_limit_bytes=VMEM_LIMIT),
        name="out_ln",
    )(og, x, wo, g, b)


def _moba_proj_kernel(x_ref, wk_ref, wvT_ref, wqT_ref, wgT_ref, ck_ref, s1_ref, s2_ref, cT_ref, sT_ref,
                      k_ref, vT_ref, qT_ref, gT_ref, sb_ref, kmt_ref):
    i = pl.program_id(1)
    nb = kmt_ref.shape[0] // H

    @pl.when(i == 0)
    def _():
        kmt_ref[...] = jnp.zeros_like(kmt_ref)

    xb = x_ref[0].astype(BF16)
    k = _dot(xb, wk_ref[...])
    parts = []
    for c in range(H * HD_B // LANES):
        kc = k[:, c * LANES:(c + 1) * LANES]
        parts.append(kc * ck_ref[...] + pltpu.roll(kc, LANES - HD_B // 2, 1) * s1_ref[...]
                     + pltpu.roll(kc, HD_B // 2, 1) * s2_ref[...])
    k_r = jnp.concatenate(parts, axis=1)
    k_ref[0] = k_r.astype(BF16)
    vT_ref[0] = _dot_nt(wvT_ref[...], xb).astype(BF16)
    gT_ref[0] = _silu(_dot_nt(wgT_ref[...], xb)).astype(BF16)

    qT = _dot_nt(wqT_ref[...], xb).reshape(H, HD_B, TM)
    rot = jnp.concatenate([-qT[:, HD_B // 2:], qT[:, :HD_B // 2]], axis=1)
    q_r = (qT * cT_ref[...][None] + rot * sT_ref[...][None]).reshape(H * HD_B, TM).astype(BF16)
    qT_ref[0] = q_r

    g = _dot(kmt_ref[...].astype(BF16), q_r).reshape(nb, H, TM)
    g_eff = [g[j] + jnp.where(j < i, 0.0, NEG) for j in range(nb)]
    rows = []
    for j in range(nb):
        rank = jnp.zeros((H, TM), F32)
        for j2 in range(nb):
            if j2 == j:
                continue
            beats = (g_eff[j2] >= g_eff[j]) if j2 < j else (g_eff[j2] > g_eff[j])
            rank = rank + jnp.where(beats, 1.0, 0.0)
        rows.append(jnp.where(rank < float(MOBA_TOPK), jnp.where(j < i, 0.0, NEG), NEG))
    sb_ref[0] = jnp.concatenate(rows, axis=0)

    k_mean = jnp.sum(k_r, axis=0, keepdims=True) * (1.0 / TM)
    head_of_lane = lax.broadcasted_iota(jnp.int32, (H, H * HD_B), 1) // HD_B
    head_of_row = lax.broadcasted_iota(jnp.int32, (H, H * HD_B), 0)
    kmt_ref[pl.ds(pl.multiple_of(i * H, H), H), :] = jnp.where(
        head_of_lane == head_of_row, jnp.broadcast_to(k_mean, (H, H * HD_B)), 0.0)


def _moba_proj(x, wk, wvT, wqT, wgT, ck, s1, s2, cT, sT):
    B, S, D = x.shape
    n = S // TM
    W = H * HD_B
    full = lambda a: pl.BlockSpec(a.shape, lambda b, i: (0,) * a.ndim)
    return pl.pallas_call(
        _moba_proj_kernel,
        grid=(B, n),
        in_specs=[
            pl.BlockSpec((1, TM, D), lambda b, i: (b, i, 0)),
            full(wk), full(wvT), full(wqT), full(wgT),
            pl.BlockSpec((TM, LANES), lambda b, i: (i, 0)),
            pl.BlockSpec((TM, LANES), lambda b, i: (i, 0)),
            pl.BlockSpec((TM, LANES), lambda b, i: (i, 0)),
            pl.BlockSpec((HD_B, TM), lambda b, i: (0, i)),
            pl.BlockSpec((HD_B, TM), lambda b, i: (0, i)),
        ],
        out_specs=[
            pl.BlockSpec((1, TM, W), lambda b, i: (b, i, 0)),
            pl.BlockSpec((1, W, TM), lambda b, i: (b, 0, i)),
            pl.BlockSpec((1, W, TM), lambda b, i: (b, 0, i)),
            pl.BlockSpec((1, W, TM), lambda b, i: (b, 0, i)),
            pl.BlockSpec((1, n * H, TM), lambda b, i: (b, 0, i)),
        ],
        out_shape=[
            jax.ShapeDtypeStruct((B, S, W), BF16),
            jax.ShapeDtypeStruct((B, W, S), BF16),
            jax.ShapeDtypeStruct((B, W, S), BF16),
            jax.ShapeDtypeStruct((B, W, S), BF16),
            jax.ShapeDtypeStruct((B, n * H, S), F32),
        ],
        scratch_shapes=[pltpu.VMEM((n * H, W), F32)],
        compiler_params=pltpu.CompilerParams(
            dimension_semantics=("arbitrary", "arbitrary"), vmem_limit_bytes=VMEM_LIMIT),
        name="moba_proj",
    )(x, wk, wvT, wqT, wgT, ck, s1, s2, cT, sT)


def _moba_attn_kernel(qT_ref, k_ref, vT_ref, gT_ref, sb_ref, og_ref):
    S = k_ref.shape[1]
    pair = pl.program_id(1)
    zeros = jnp.zeros((HD_B, TM), BF16)

    def scores(qi, hh):
        q = qT_ref[0, hh * HD_B:(hh + 1) * HD_B, qi * TM:(qi + 1) * TM]
        q_slab = jnp.concatenate([q, zeros] if hh == 0 else [zeros, q], axis=0)
        return _dot(k_ref[0, :(qi + 1) * TM, :], q_slab)

    def weights(qi, hh, s):
        cols = slice(qi * TM, (qi + 1) * TM)
        bias = [sb_ref[0, pl.ds(j * H + 2 * pair + hh, 1), cols] for j in range(qi)]
        return _softmax_weights(s, qi, bias)

    def values(qi, hh, p):
        return _weighted_values(vT_ref[0, hh * HD_B:(hh + 1) * HD_B, :(qi + 1) * TM], p)

    _attend_pair(S // TM, scores, weights, values, gT_ref, og_ref)


def _moba_attn(qT, k, vT, gT, sb):
    B, W, S = qT.shape
    return pl.pallas_call(
        _moba_attn_kernel,
        grid=(B, H // 2),
        in_specs=[
            pl.BlockSpec((1, 2 * HD_B, S), lambda b, p: (b, p, 0)),
            pl.BlockSpec((1, S, 2 * HD_B), lambda b, p: (b, 0, p)),
            pl.BlockSpec((1, 2 * HD_B, S), lambda b, p: (b, p, 0)),
            pl.BlockSpec((1, 2 * HD_B, S), lambda b, p: (b, p, 0)),
            pl.BlockSpec((1, sb.shape[1], S), lambda b, p: (b, 0, 0)),
        ],
        out_specs=pl.BlockSpec((1, S, 2 * HD_B), lambda b, p: (b, 0, p)),
        out_shape=jax.ShapeDtypeStruct((B, S, W), BF16),
        compiler_params=pltpu.CompilerParams(
            dimension_semantics=("arbitrary", "arbitrary"), vmem_limit_bytes=VMEM_LIMIT),
        name="moba_attn",
    )(qT, k, vT, gT, sb)


def _rope_tables(seq, dim):
    inv = THETA ** (-jnp.arange(0, dim, 2, dtype=F32) / dim)
    ang = jnp.arange(seq, dtype=F32)[:, None] * inv[None, :]
    ang = jnp.concatenate([ang, ang], axis=-1)
    return jnp.cos(ang), jnp.sin(ang)


def _rot_cols(w):
    half = w.shape[-1] // 2
    return jnp.concatenate([-w[..., half:], w[..., :half]], axis=-1)


def _mla_layer(x, w_in, q_norm, kv_norm, w_uq, w_ukv, w_o, ln_g, ln_b, cos, sin):
    B, S, D = x.shape
    scale = (NOPE + ROPE) ** -0.5 * LOG2E
    w_cq = w_in[:, :Q_LORA]
    w_ckv = w_in[:, Q_LORA:Q_LORA + KV_LORA]
    w_kr = w_in[:, Q_LORA + KV_LORA:Q_LORA + KV_LORA + ROPE]
    w_g = w_in[:, Q_LORA + KV_LORA + ROPE:]
    w_kr_rot = _rot_cols(w_kr)
    wa = jnp.concatenate([w_cq, w_ckv, w_kr, w_kr_rot, w_kr, w_kr_rot], axis=1).astype(BF16)

    wq = w_uq.reshape(Q_LORA, H, NOPE + ROPE)
    wq_n, wq_r = wq[..., :NOPE], wq[..., NOPE:]
    wq_rr = _rot_cols(wq_r)
    even = jnp.concatenate([wq_n, wq_r, wq_rr], axis=-1)
    odd = jnp.concatenate([wq_r, wq_rr, wq_n], axis=-1)
    is_even = (jnp.arange(H) % 2 == 0)[None, :, None]
    wqT = jnp.where(is_even, even, odd).reshape(Q_LORA, H * SLAB).T.astype(BF16)

    wkv = w_ukv.reshape(KV_LORA, H, NOPE + V_HD)
    wkn = wkv[..., :NOPE].reshape(KV_LORA, H * NOPE).astype(BF16)
    wvT = wkv[..., NOPE:].reshape(KV_LORA, H * V_HD).T.astype(BF16)
    wgT = w_g.T.astype(BF16)

    cT, sT = cos.T * scale, sin.T * scale
    ones = jnp.full((NOPE, S), scale, F32)
    tq = jnp.concatenate([ones, cT, sT, cT, sT, ones], axis=0)
    tkr = jnp.concatenate([cos, sin, cos, sin], axis=1)

    qT, kn, kr, vT, gT = _mla_proj(x, wa, q_norm[None], kv_norm[None], wqT, wkn, wvT, wgT, tq, tkr)
    og = _mla_attn(qT, kn, kr, vT, gT)
    out = _out_ln(og.reshape(B * S, H * V_HD), x.reshape(B * S, D), w_o.astype(BF16), ln_g[None], ln_b[None])
    return out.reshape(B, S, D)


def _moba_layer(x, w_kv, w_in, w_o, ln_g, ln_b, cos, sin):
    B, S, D = x.shape
    W = H * HD_B
    scale = HD_B ** -0.5 * LOG2E
    half = HD_B // 2
    wk = w_kv[:, :W].astype(BF16)
    wvT = w_kv[:, W:].T.astype(BF16)
    wqT = w_in[:, :W].T.astype(BF16)
    wgT = w_in[:, W:].T.astype(BF16)
    zeros = jnp.zeros((S, half), F32)
    ck = jnp.concatenate([cos, cos], axis=1)
    s1 = jnp.tile(jnp.concatenate([-sin[:, :half], zeros], axis=1), (1, 2))
    s2 = jnp.tile(jnp.concatenate([zeros, sin[:, half:]], axis=1), (1, 2))
    cT, sT = cos.T * scale, sin.T * scale
    k, vT, qT, gT, sb = _moba_proj(x, wk, wvT, wqT, wgT, ck, s1, s2, cT, sT)
    og = _moba_attn(qT, k, vT, gT, sb)
    out = _out_ln(og.reshape(B * S, W), x.reshape(B * S, D), w_o.astype(BF16), ln_g[None], ln_b[None])
    return out.reshape(B, S, D)


def kernel(x, mla_w_in, mla_q_norm, mla_kv_norm, mla_w_uq, mla_w_ukv, mla_w_o,
           moba_w_kv, moba_w_in, moba_w_o, ln_g, ln_b):
    S = x.shape[1]
    n_a = mla_w_in.shape[0]
    n_b = moba_w_in.shape[0]
    cos_a, sin_a = _rope_tables(S, ROPE)
    cos_b, sin_b = _rope_tables(S, HD_B)
    for l in range(n_a):
        x = _mla_layer(x, mla_w_in[l], mla_q_norm[l], mla_kv_norm[l], mla_w_uq[l], mla_w_ukv[l], mla_w_o[l],
                       ln_g[l], ln_b[l], cos_a, sin_a)
    for j in range(n_b):
        assert n_b == 1
        x = _moba_layer(x, moba_w_kv, moba_w_in[j], moba_w_o[j], ln_g[n_a + j], ln_b[n_a + j], cos_b, sin_b)
    return x
```

```python
import jax
import jax.numpy as jnp
from jax import lax
from jax.experimental import pallas as pl
from jax.experimental.pallas import tpu as pltpu

D_MODEL = 1024
H = 16
NOPE = 64
ROPE = 32
V_HD = 64
Q_LORA = 768
KV_LORA = 256
HD_B = 64
MOBA_BLOCK = 256
MOBA_TOPK = 3
THETA = 10000.0
LN_EPS = 1e-5
RMS_EPS = 1e-6
DEPTH = 2
ALPHA = (2 * DEPTH) ** 0.25

TM = MOBA_BLOCK
TM_OUT = 1024
LN_ROWS = 256
PAIRS_PER_STEP = 2
LOOKAHEAD = 2
LANES = 128
BF16_ROWS = 16
SLAB = LANES
PAIR = 2 * SLAB
LOG2E = 1.4426950408889634
NEG = -1e30
VMEM_LIMIT = 48 * 1024 * 1024

F32 = jnp.float32
BF16 = jnp.bfloat16
_NT = (((1,), (1,)), ((), ()))


def _dot(a, b):
    return jnp.dot(a, b, preferred_element_type=F32)


def _dot_nt(a, b):
    return lax.dot_general(a, b, _NT, preferred_element_type=F32)


def _rms_norm(c, gain):
    return c * lax.rsqrt(jnp.mean(c * c, axis=-1, keepdims=True) + RMS_EPS) * gain


def _silu(g):
    return g * jax.nn.sigmoid(g)


def _mla_proj_kernel(x_ref, wa_ref, qn_ref, kvn_ref, wqT_ref, wkn_ref, wvT_ref, wgT_ref, tq_ref, tkr_ref,
                     qT_ref, kn_ref, kr_ref, vT_ref, gT_ref):
    xb = x_ref[0].astype(BF16)
    h = _dot(xb, wa_ref[...])
    gT_ref[0] = _silu(_dot_nt(wgT_ref[...], xb)).astype(BF16)
    cq_n = _rms_norm(h[:, :Q_LORA], qn_ref[...]).astype(BF16)
    ckv_n = _rms_norm(h[:, Q_LORA:Q_LORA + KV_LORA], kvn_ref[...]).astype(BF16)
    qT = _dot_nt(wqT_ref[...], cq_n)
    qT = qT.reshape(H // 2, PAIR, TM) * tq_ref[...][None]
    qT_ref[0] = qT.reshape(H * SLAB, TM).astype(BF16)
    kn_ref[0] = _dot(ckv_n, wkn_ref[...]).astype(BF16)
    vT_ref[0] = _dot_nt(wvT_ref[...], ckv_n).astype(BF16)
    t = h[:, Q_LORA + KV_LORA:] * tkr_ref[...]
    kr_ref[0] = (t + pltpu.roll(t, ROPE, 1)).astype(BF16)


def _mla_proj(x, wa, qn, kvn, wqT, wkn, wvT, wgT, tq, tkr):
    B, S, D = x.shape
    n = S // TM
    full = lambda a: pl.BlockSpec(a.shape, lambda b, i: (0,) * a.ndim)
    return pl.pallas_call(
        _mla_proj_kernel,
        grid=(B, n),
        in_specs=[
            pl.BlockSpec((1, TM, D), lambda b, i: (b, i, 0)),
            full(wa), full(qn), full(kvn), full(wqT), full(wkn), full(wvT), full(wgT),
            pl.BlockSpec((PAIR, TM), lambda b, i: (0, i)),
            pl.BlockSpec((TM, LANES), lambda b, i: (i, 0)),
        ],
        out_specs=[
            pl.BlockSpec((1, H * SLAB, TM), lambda b, i: (b, 0, i)),
            pl.BlockSpec((1, TM, H * NOPE), lambda b, i: (b, i, 0)),
            pl.BlockSpec((1, TM, LANES), lambda b, i: (b, i, 0)),
            pl.BlockSpec((1, H * V_HD, TM), lambda b, i: (b, 0, i)),
            pl.BlockSpec((1, H * V_HD, TM), lambda b, i: (b, 0, i)),
        ],
        out_shape=[
            jax.ShapeDtypeStruct((B, H * SLAB, S), BF16),
            jax.ShapeDtypeStruct((B, S, H * NOPE), BF16),
            jax.ShapeDtypeStruct((B, S, LANES), BF16),
            jax.ShapeDtypeStruct((B, H * V_HD, S), BF16),
            jax.ShapeDtypeStruct((B, H * V_HD, S), BF16),
        ],
        compiler_params=pltpu.CompilerParams(
            dimension_semantics=("arbitrary", "arbitrary"), vmem_limit_bytes=VMEM_LIMIT),
        name="mla_proj",
    )(x, wa, qn, kvn, wqT, wkn, wvT, wgT, tq, tkr)


def _causal_mask():
    r = lax.broadcasted_iota(jnp.int32, (TM, TM), 0)
    c = lax.broadcasted_iota(jnp.int32, (TM, TM), 1)
    return r <= c


def _attend_pairs(n_q, score_block, bias_row, value_block, gT_ref, og_ref):
    mask = _causal_mask()
    steps = []
    for pl_ in range(PAIRS_PER_STEP):
        for qi in range(n_q):
            blocks = [qi] + list(range(qi))
            steps += [(pl_, qi, j, t == 0, t == qi) for t, j in enumerate(blocks)]

    def step_scores(step):
        pl_, qi, j = step[:3]
        return [score_block(pl_, qi, hh, j) for hh in range(2)]

    ahead = [step_scores(st) for st in steps[:LOOKAHEAD]]
    m = [None, None]
    acc = [None, None]
    for n, (pl_, qi, j, first, last) in enumerate(steps):
        s_cur = ahead.pop(0)
        if n + LOOKAHEAD < len(steps):
            ahead.append(step_scores(steps[n + LOOKAHEAD]))
        for hh in range(2):
            if first:
                s = jnp.where(mask, s_cur[hh], NEG)
                m[hh] = jnp.max(s, axis=0, keepdims=True)
                p = jnp.exp2(s - m[hh]).astype(BF16)
                acc[hh] = _dot(value_block(pl_, qi, hh, j), p)
            else:
                s = s_cur[hh]
                b = bias_row(pl_, qi, hh, j)
                cm = jnp.max(s, axis=0, keepdims=True)
                m_new = jnp.maximum(m[hh], cm if b is None else cm + b)
                p = jnp.exp2(s - (m_new if b is None else m_new - b)).astype(BF16)
                acc[hh] = acc[hh] * jnp.exp2(m[hh] - m_new) + _dot(value_block(pl_, qi, hh, j), p)
                m[hh] = m_new
        if last:
            hd = acc[0].shape[0] - BF16_ROWS
            outs = [acc[hh][:hd] / acc[hh][hd:hd + 1] for hh in range(2)]
            cols = slice(qi * TM, (qi + 1) * TM)
            feats = slice(pl_ * LANES, (pl_ + 1) * LANES)
            oT = jnp.concatenate(outs, axis=0) * gT_ref[0, feats, cols].astype(F32)
            og_ref[0, cols, feats] = oT.T.astype(BF16)


def _mla_attn_kernel(qT_ref, kn_ref, kr_ref, vT_ref, gT_ref, og_ref, kslab_ref):
    S = kn_ref.shape[1]
    lane = lax.broadcasted_iota(jnp.int32, (S, SLAB), 1)
    kr = kr_ref[0]
    for pl_ in range(PAIRS_PER_STEP):
        kn = kn_ref[0, :, pl_ * LANES:(pl_ + 1) * LANES]
        kslab_ref[2 * pl_] = jnp.where(lane < NOPE, kn, kr)
        kslab_ref[2 * pl_ + 1] = jnp.where(lane >= NOPE, kn, kr)

    ones = jnp.ones((BF16_ROWS, TM), BF16)

    def score_block(pl_, qi, hh, j):
        h = 2 * pl_ + hh
        q_slab = qT_ref[0, h * SLAB:(h + 1) * SLAB, qi * TM:(qi + 1) * TM]
        return _dot(kslab_ref[h, j * TM:(j + 1) * TM, :], q_slab)

    def bias_row(pl_, qi, hh, j):
        return None

    def value_block(pl_, qi, hh, j):
        h = 2 * pl_ + hh
        return jnp.concatenate([vT_ref[0, h * V_HD:(h + 1) * V_HD, j * TM:(j + 1) * TM], ones], axis=0)

    _attend_pairs(S // TM, score_block, bias_row, value_block, gT_ref, og_ref)


def _mla_attn(qT, kn, kr, vT, gT):
    B, _, S = qT.shape
    hs = 2 * PAIRS_PER_STEP
    return pl.pallas_call(
        _mla_attn_kernel,
        grid=(B, H // hs),
        in_specs=[
            pl.BlockSpec((1, hs * SLAB, S), lambda b, p: (b, p, 0)),
            pl.BlockSpec((1, S, hs * NOPE), lambda b, p: (b, 0, p)),
            pl.BlockSpec((1, S, LANES), lambda b, p: (b, 0, 0)),
            pl.BlockSpec((1, hs * V_HD, S), lambda b, p: (b, p, 0)),
            pl.BlockSpec((1, hs * V_HD, S), lambda b, p: (b, p, 0)),
        ],
        out_specs=pl.BlockSpec((1, S, hs * V_HD), lambda b, p: (b, 0, p)),
        out_shape=jax.ShapeDtypeStruct((B, S, H * V_HD), BF16),
        scratch_shapes=[pltpu.VMEM((hs, S, SLAB), BF16)],
        compiler_params=pltpu.CompilerParams(
            dimension_semantics=("arbitrary", "arbitrary"), vmem_limit_bytes=VMEM_LIMIT),
        name="mla_attn",
    )(qT, kn, kr, vT, gT)


def _out_ln_kernel(og_ref, x_ref, wo_ref, g_ref, b_ref, o_ref):
    for r in range(TM_OUT // LN_ROWS):
        rows = slice(r * LN_ROWS, (r + 1) * LN_ROWS)
        z = ALPHA * x_ref[rows, :] + _dot(og_ref[rows, :], wo_ref[...])
        mu = jnp.mean(z, axis=-1, keepdims=True)
        zc = z - mu
        var = jnp.mean(zc * zc, axis=-1, keepdims=True)
        o_ref[rows, :] = zc * lax.rsqrt(var + LN_EPS) * g_ref[...] + b_ref[...]


def _out_ln(og, x, wo, g, b):
    R, D = x.shape
    W = og.shape[1]
    return pl.pallas_call(
        _out_ln_kernel,
        grid=(R // TM_OUT,),
        in_specs=[
            pl.BlockSpec((TM_OUT, W), lambda i: (i, 0)),
            pl.BlockSpec((TM_OUT, D), lambda i: (i, 0)),
            pl.BlockSpec((W, D), lambda i: (0, 0)),
            pl.BlockSpec((1, D), lambda i: (0, 0)),
            pl.BlockSpec((1, D), lambda i: (0, 0)),
        ],
        out_specs=pl.BlockSpec((TM_OUT, D), lambda i: (i, 0)),
        out_shape=jax.ShapeDtypeStruct((R, D), F32),
        compiler_params=pltpu.CompilerParams(
            dimension_semantics=("arbitrary",), vmem_limit_bytes=VMEM_LIMIT),
        name="out_ln",
    )(og, x, wo, g, b)


def _moba_proj_kernel(x_ref, wk_ref, wvT_ref, wqT_ref, wgT_ref, ck_ref, s1_ref, s2_ref, cT_ref, sT_ref,
                      k_ref, vT_ref, qT_ref, gT_ref, sb_ref, kmt_ref):
    i = pl.program_id(1)
    nb = kmt_ref.shape[0] // H

    @pl.when(i == 0)
    def _():
        kmt_ref[...] = jnp.zeros_like(kmt_ref)

    xb = x_ref[0].astype(BF16)
    qT = _dot_nt(wqT_ref[...], xb).reshape(H, HD_B, TM)
    k = _dot(xb, wk_ref[...])
    rot = jnp.concatenate([-qT[:, HD_B // 2:], qT[:, :HD_B // 2]], axis=1)
    q_r = (qT * cT_ref[...][None] + rot * sT_ref[...][None]).reshape(H * HD_B, TM).astype(BF16)
    qT_ref[0] = q_r

    g = _dot(kmt_ref[...].astype(BF16), q_r).reshape(nb, H, TM)
    gT_ref[0] = _silu(_dot_nt(wgT_ref[...], xb)).astype(BF16)
    vT_ref[0] = _dot_nt(wvT_ref[...], xb).astype(BF16)

    parts = []
    for c in range(H * HD_B // LANES):
        kc = k[:, c * LANES:(c + 1) * LANES]
        parts.append(kc * ck_ref[...] + pltpu.roll(kc, LANES - HD_B // 2, 1) * s1_ref[...]
                     + pltpu.roll(kc, HD_B // 2, 1) * s2_ref[...])
    k_r = jnp.concatenate(parts, axis=1)
    k_ref[0] = k_r.astype(BF16)

    g_eff = [g[j] + jnp.where(j < i, 0.0, NEG) for j in range(nb)]
    rows = []
    for j in range(nb):
        rank = jnp.zeros((H, TM), F32)
        for j2 in range(nb):
            if j2 == j:
                continue
            beats = (g_eff[j2] >= g_eff[j]) if j2 < j else (g_eff[j2] > g_eff[j])
            rank = rank + jnp.where(beats, 1.0, 0.0)
        rows.append(jnp.where(rank < float(MOBA_TOPK), jnp.where(j < i, 0.0, NEG), NEG))
    sb_ref[0] = jnp.concatenate(rows, axis=0)

    k_mean = jnp.sum(k_r, axis=0, keepdims=True) * (1.0 / TM)
    head_of_lane = lax.broadcasted_iota(jnp.int32, (H, H * HD_B), 1) // HD_B
    head_of_row = lax.broadcasted_iota(jnp.int32, (H, H * HD_B), 0)
    kmt_ref[pl.ds(pl.multiple_of(i * H, H), H), :] = jnp.where(
        head_of_lane == head_of_row, jnp.broadcast_to(k_mean, (H, H * HD_B)), 0.0)


def _moba_proj(x, wk, wvT, wqT, wgT, ck, s1, s2, cT, sT):
    B, S, D = x.shape
    n = S // TM
    W = H * HD_B
    full = lambda a: pl.BlockSpec(a.shape, lambda b, i: (0,) * a.ndim)
    return pl.pallas_call(
        _moba_proj_kernel,
        grid=(B, n),
        in_specs=[
            pl.BlockSpec((1, TM, D), lambda b, i: (b, i, 0)),
            full(wk), full(wvT), full(wqT), full(wgT),
            pl.BlockSpec((TM, LANES), lambda b, i: (i, 0)),
            pl.BlockSpec((TM, LANES), lambda b, i: (i, 0)),
            pl.BlockSpec((TM, LANES), lambda b, i: (i, 0)),
            pl.BlockSpec((HD_B, TM), lambda b, i: (0, i)),
            pl.BlockSpec((HD_B, TM), lambda b, i: (0, i)),
        ],
        out_specs=[
            pl.BlockSpec((1, TM, W), lambda b, i: (b, i, 0)),
            pl.BlockSpec((1, W, TM), lambda b, i: (b, 0, i)),
            pl.BlockSpec((1, W, TM), lambda b, i: (b, 0, i)),
            pl.BlockSpec((1, W, TM), lambda b, i: (b, 0, i)),
            pl.BlockSpec((1, n * H, TM), lambda b, i: (b, 0, i)),
        ],
        out_shape=[
            jax.ShapeDtypeStruct((B, S, W), BF16),
            jax.ShapeDtypeStruct((B, W, S), BF16),
            jax.ShapeDtypeStruct((B, W, S), BF16),
            jax.ShapeDtypeStruct((B, W, S), BF16),
            jax.ShapeDtypeStruct((B, n * H, S), F32),
        ],
        scratch_shapes=[pltpu.VMEM((n * H, W), F32)],
        compiler_params=pltpu.CompilerParams(
            dimension_semantics=("arbitrary", "arbitrary"), vmem_limit_bytes=VMEM_LIMIT),
        name="moba_proj",
    )(x, wk, wvT, wqT, wgT, ck, s1, s2, cT, sT)


def _moba_attn_kernel(qT_ref, k_ref, vT_ref, gT_ref, sb_ref, og_ref):
    S = k_ref.shape[1]
    first_head = 2 * PAIRS_PER_STEP * pl.program_id(1)
    zeros = jnp.zeros((HD_B, TM), BF16)

    ones = jnp.ones((BF16_ROWS, TM), BF16)

    def score_block(pl_, qi, hh, j):
        h = 2 * pl_ + hh
        q = qT_ref[0, h * HD_B:(h + 1) * HD_B, qi * TM:(qi + 1) * TM]
        q_slab = jnp.concatenate([q, zeros] if hh == 0 else [zeros, q], axis=0)
        return _dot(k_ref[0, j * TM:(j + 1) * TM, pl_ * LANES:(pl_ + 1) * LANES], q_slab)

    def bias_row(pl_, qi, hh, j):
        head = first_head + 2 * pl_ + hh
        return sb_ref[0, pl.ds(j * H + head, 1), qi * TM:(qi + 1) * TM]

    def value_block(pl_, qi, hh, j):
        h = 2 * pl_ + hh
        return jnp.concatenate([vT_ref[0, h * HD_B:(h + 1) * HD_B, j * TM:(j + 1) * TM], ones], axis=0)

    _attend_pairs(S // TM, score_block, bias_row, value_block, gT_ref, og_ref)


def _moba_attn(qT, k, vT, gT, sb):
    B, W, S = qT.shape
    hs = 2 * PAIRS_PER_STEP
    return pl.pallas_call(
        _moba_attn_kernel,
        grid=(B, H // hs),
        in_specs=[
            pl.BlockSpec((1, hs * HD_B, S), lambda b, p: (b, p, 0)),
            pl.BlockSpec((1, S, hs * HD_B), lambda b, p: (b, 0, p)),
            pl.BlockSpec((1, hs * HD_B, S), lambda b, p: (b, p, 0)),
            pl.BlockSpec((1, hs * HD_B, S), lambda b, p: (b, p, 0)),
            pl.BlockSpec((1, sb.shape[1], S), lambda b, p: (b, 0, 0)),
        ],
        out_specs=pl.BlockSpec((1, S, hs * HD_B), lambda b, p: (b, 0, p)),
        out_shape=jax.ShapeDtypeStruct((B, S, W), BF16),
        compiler_params=pltpu.CompilerParams(
            dimension_semantics=("arbitrary", "arbitrary"), vmem_limit_bytes=VMEM_LIMIT),
        name="moba_attn",
    )(qT, k, vT, gT, sb)


def _rope_tables(seq, dim):
    inv = THETA ** (-jnp.arange(0, dim, 2, dtype=F32) / dim)
    ang = jnp.arange(seq, dtype=F32)[:, None] * inv[None, :]
    ang = jnp.concatenate([ang, ang], axis=-1)
    return jnp.cos(ang), jnp.sin(ang)


def _rot_cols(w):
    half = w.shape[-1] // 2
    return jnp.concatenate([-w[..., half:], w[..., :half]], axis=-1)


def _mla_layer(x, w_in, q_norm, kv_norm, w_uq, w_ukv, w_o, ln_g, ln_b, cos, sin):
    B, S, D = x.shape
    scale = (NOPE + ROPE) ** -0.5 * LOG2E
    w_cq = w_in[:, :Q_LORA]
    w_ckv = w_in[:, Q_LORA:Q_LORA + KV_LORA]
    w_kr = w_in[:, Q_LORA + KV_LORA:Q_LORA + KV_LORA + ROPE]
    w_g = w_in[:, Q_LORA + KV_LORA + ROPE:]
    w_kr_rot = _rot_cols(w_kr)
    wa = jnp.concatenate([w_cq, w_ckv, w_kr, w_kr_rot, w_kr, w_kr_rot], axis=1).astype(BF16)

    wq = w_uq.reshape(Q_LORA, H, NOPE + ROPE)
    wq_n, wq_r = wq[..., :NOPE], wq[..., NOPE:]
    wq_rr = _rot_cols(wq_r)
    even = jnp.concatenate([wq_n, wq_r, wq_rr], axis=-1)
    odd = jnp.concatenate([wq_r, wq_rr, wq_n], axis=-1)
    is_even = (jnp.arange(H) % 2 == 0)[None, :, None]
    wqT = jnp.where(is_even, even, odd).reshape(Q_LORA, H * SLAB).T.astype(BF16)

    wkv = w_ukv.reshape(KV_LORA, H, NOPE + V_HD)
    wkn = wkv[..., :NOPE].reshape(KV_LORA, H * NOPE).astype(BF16)
    wvT = wkv[..., NOPE:].reshape(KV_LORA, H * V_HD).T.astype(BF16)
    wgT = w_g.T.astype(BF16)

    cT, sT = cos.T * scale, sin.T * scale
    ones = jnp.full((NOPE, S), scale, F32)
    tq = jnp.concatenate([ones, cT, sT, cT, sT, ones], axis=0)
    tkr = jnp.concatenate([cos, sin, cos, sin], axis=1)

    qT, kn, kr, vT, gT = _mla_proj(x, wa, q_norm[None], kv_norm[None], wqT, wkn, wvT, wgT, tq, tkr)
    og = _mla_attn(qT, kn, kr, vT, gT)
    out = _out_ln(og.reshape(B * S, H * V_HD), x.reshape(B * S, D), w_o.astype(BF16), ln_g[None], ln_b[None])
    return out.reshape(B, S, D)


def _moba_layer(x, w_kv, w_in, w_o, ln_g, ln_b, cos, sin):
    B, S, D = x.shape
    W = H * HD_B
    scale = HD_B ** -0.5 * LOG2E
    half = HD_B // 2
    wk = w_kv[:, :W].astype(BF16)
    wvT = w_kv[:, W:].T.astype(BF16)
    wqT = w_in[:, :W].T.astype(BF16)
    wgT = w_in[:, W:].T.astype(BF16)
    zeros = jnp.zeros((S, half), F32)
    ck = jnp.concatenate([cos, cos], axis=1)
    s1 = jnp.tile(jnp.concatenate([-sin[:, :half], zeros], axis=1), (1, 2))
    s2 = jnp.tile(jnp.concatenate([zeros, sin[:, half:]], axis=1), (1, 2))
    cT, sT = cos.T * scale, sin.T * scale
    k, vT, qT, gT, sb = _moba_proj(x, wk, wvT, wqT, wgT, ck, s1, s2, cT, sT)
    og = _moba_attn(qT, k, vT, gT, sb)
    out = _out_ln(og.reshape(B * S, W), x.reshape(B * S, D), w_o.astype(BF16), ln_g[None], ln_b[None])
    return out.reshape(B, S, D)


def kernel(x, mla_w_in, mla_q_norm, mla_kv_norm, mla_w_uq, mla_w_ukv, mla_w_o,
           moba_w_kv, moba_w_in, moba_w_o, ln_g, ln_b):
    S = x.shape[1]
    n_a = mla_w_in.shape[0]
    n_b = moba_w_in.shape[0]
    cos_a, sin_a = _rope_tables(S, ROPE)
    cos_b, sin_b = _rope_tables(S, HD_B)
    for l in range(n_a):
        x = _mla_layer(x, mla_w_in[l], mla_q_norm[l], mla_kv_norm[l], mla_w_uq[l], mla_w_ukv[l], mla_w_o[l],
                       ln_g[l], ln_b[l], cos_a, sin_a)
    for j in range(n_b):
        assert n_b == 1
        x = _moba_layer(x, moba_w_kv, moba_w_in[j], moba_w_o[j], ln_g[n_a + j], ln_b[n_a + j], cos_b, sin_b)
    return x
```

```python
import jax
import jax.numpy as jnp
from jax import lax
from jax.experimental import pallas as pl
from jax.experimental.pallas import tpu as pltpu

D_MODEL = 1024
H = 16
NOPE = 64
ROPE = 32
V_HD = 64
Q_LORA = 768
KV_LORA = 256
HD_B = 64
MOBA_BLOCK = 256
MOBA_TOPK = 3
THETA = 10000.0
LN_EPS = 1e-5
RMS_EPS = 1e-6
DEPTH = 2
ALPHA = (2 * DEPTH) ** 0.25

TM = MOBA_BLOCK
TM_OUT = 1024
LN_ROWS = 256
PAIRS_PER_STEP = 2
LOOKAHEAD = 2
LANES = 128
BF16_ROWS = 16
SLAB = LANES
QK_ROWS = NOPE + ROPE
LOG2E = 1.4426950408889634
MLA_SCALE = QK_ROWS ** -0.5 * LOG2E
MOBA_SCALE = HD_B ** -0.5 * LOG2E
NEG = -1e30
VMEM_LIMIT = 48 * 1024 * 1024

F32 = jnp.float32
BF16 = jnp.bfloat16
_NT = (((1,), (1,)), ((), ()))


def _dot(a, b):
    return jnp.dot(a, b, preferred_element_type=F32)


def _dot_nt(a, b):
    return lax.dot_general(a, b, _NT, preferred_element_type=F32)


def _rms_norm(c, gain):
    return c * lax.rsqrt(jnp.mean(c * c, axis=-1, keepdims=True) + RMS_EPS) * gain


def _silu(g):
    return g * jax.nn.sigmoid(g)


def _rot_rows(t):
    half = t.shape[-2] // 2
    return jnp.concatenate([-t[..., half:, :], t[..., :half, :]], axis=-2)


def _mla_proj_kernel(x_ref, wa_ref, qn_ref, kvn_ref, wqT_ref, wkn_ref, wvT_ref, wgkT_ref, tq_ref, tk_ref,
                     qT_ref, kn_ref, kr_ref, vT_ref, gT_ref):
    xb = x_ref[0].astype(BF16)
    c_kv = _dot(xb, wa_ref[:, Q_LORA:])
    c_q = _dot(xb, wa_ref[:, :Q_LORA])
    gk = _dot_nt(wgkT_ref[...], xb)
    gT_ref[0] = _silu(gk[:H * V_HD]).astype(BF16)
    ckv_n = _rms_norm(c_kv, kvn_ref[...]).astype(BF16)
    cq_n = _rms_norm(c_q, qn_ref[...]).astype(BF16)
    kn_ref[0] = _dot(ckv_n, wkn_ref[...]).astype(BF16)
    vT_ref[0] = _dot_nt(wvT_ref[...], ckv_n).astype(BF16)
    qT = _dot_nt(wqT_ref[...], cq_n).reshape(H // 2, 2, QK_ROWS, TM)

    kr = gk[H * V_HD:]
    kr = kr * tk_ref[:ROPE, :] + _rot_rows(kr) * tk_ref[ROPE:, :]
    kr_ref[0] = jnp.concatenate([kr] * (LANES // ROPE), axis=0).T.astype(BF16)

    nope = qT[:, :, :NOPE] * MLA_SCALE
    rope = qT[:, :, NOPE:]
    roped = rope * tq_ref[:ROPE, :] + _rot_rows(rope) * tq_ref[ROPE:, :]
    pair = jnp.concatenate([nope[:, 0], roped[:, 0], roped[:, 1], nope[:, 1]], axis=1)
    qT_ref[0] = pair.reshape(H * QK_ROWS, TM).astype(BF16)


def _mla_proj(x, wa, qn, kvn, wqT, wkn, wvT, wgkT, tq, tk):
    B, S, D = x.shape
    n = S // TM
    full = lambda a: pl.BlockSpec(a.shape, lambda b, i: (0,) * a.ndim)
    return pl.pallas_call(
        _mla_proj_kernel,
        grid=(B, n),
        in_specs=[
            pl.BlockSpec((1, TM, D), lambda b, i: (b, i, 0)),
            full(wa), full(qn), full(kvn), full(wqT), full(wkn), full(wvT), full(wgkT),
            pl.BlockSpec((tq.shape[0], TM), lambda b, i: (0, i)),
            pl.BlockSpec((tk.shape[0], TM), lambda b, i: (0, i)),
        ],
        out_specs=[
            pl.BlockSpec((1, H * QK_ROWS, TM), lambda b, i: (b, 0, i)),
            pl.BlockSpec((1, TM, H * NOPE), lambda b, i: (b, i, 0)),
            pl.BlockSpec((1, TM, LANES), lambda b, i: (b, i, 0)),
            pl.BlockSpec((1, H * V_HD, TM), lambda b, i: (b, 0, i)),
            pl.BlockSpec((1, H * V_HD, TM), lambda b, i: (b, 0, i)),
        ],
        out_shape=[
            jax.ShapeDtypeStruct((B, H * QK_ROWS, S), BF16),
            jax.ShapeDtypeStruct((B, S, H * NOPE), BF16),
            jax.ShapeDtypeStruct((B, S, LANES), BF16),
            jax.ShapeDtypeStruct((B, H * V_HD, S), BF16),
            jax.ShapeDtypeStruct((B, H * V_HD, S), BF16),
        ],
        compiler_params=pltpu.CompilerParams(
            dimension_semantics=("arbitrary", "arbitrary"), vmem_limit_bytes=VMEM_LIMIT),
        name="mla_proj",
    )(x, wa, qn, kvn, wqT, wkn, wvT, wgkT, tq, tk)


def _causal_mask():
    r = lax.broadcasted_iota(jnp.int32, (TM, TM), 0)
    c = lax.broadcasted_iota(jnp.int32, (TM, TM), 1)
    return r <= c


def _attend_pairs(n_q, score_block, bias_row, value_block, gT_ref, og_ref):
    mask = _causal_mask()
    steps = []
    for pl_ in range(PAIRS_PER_STEP):
        for qi in range(n_q):
            blocks = [qi] + list(range(qi))
            steps += [(pl_, qi, j, t == 0, t == qi) for t, j in enumerate(blocks)]

    def step_scores(step):
        pl_, qi, j, first = step[:4]
        out = []
        for hh in range(2):
            s = score_block(pl_, qi, hh, j)
            if first:
                s = jnp.where(mask, s, NEG)
            out.append((s, jnp.max(s, axis=0, keepdims=True)))
        return out

    ahead = [step_scores(st) for st in steps[:LOOKAHEAD]]
    m = [None, None]
    acc = [None, None]
    for n, (pl_, qi, j, first, last) in enumerate(steps):
        s_cur = ahead.pop(0)
        if n + LOOKAHEAD < len(steps):
            ahead.append(step_scores(steps[n + LOOKAHEAD]))
        for hh in range(2):
            s, cm = s_cur[hh]
            if first:
                m[hh] = cm
                p = jnp.exp2(s - m[hh]).astype(BF16)
                acc[hh] = _dot(value_block(pl_, qi, hh, j), p)
            else:
                b = bias_row(pl_, qi, hh, j)
                m_new = jnp.maximum(m[hh], cm if b is None else cm + b)
                p = jnp.exp2(s - (m_new if b is None else m_new - b)).astype(BF16)
                acc[hh] = acc[hh] * jnp.exp2(m[hh] - m_new) + _dot(value_block(pl_, qi, hh, j), p)
                m[hh] = m_new
        if last:
            hd = acc[0].shape[0] - BF16_ROWS
            outs = [acc[hh][:hd] / acc[hh][hd:hd + 1] for hh in range(2)]
            cols = slice(qi * TM, (qi + 1) * TM)
            feats = slice(pl_ * LANES, (pl_ + 1) * LANES)
            oT = jnp.concatenate(outs, axis=0) * gT_ref[0, feats, cols].astype(F32)
            og_ref[0, cols, feats] = oT.T.astype(BF16)


def _mla_attn_kernel(qT_ref, kn_ref, kr_ref, vT_ref, gT_ref, og_ref, kslab_ref):
    S = kn_ref.shape[1]
    lane = lax.broadcasted_iota(jnp.int32, (S, SLAB), 1)
    kr = kr_ref[0]
    for pl_ in range(PAIRS_PER_STEP):
        kn = kn_ref[0, :, pl_ * LANES:(pl_ + 1) * LANES]
        kslab_ref[2 * pl_] = jnp.where(lane < NOPE, kn, kr)
        kslab_ref[2 * pl_ + 1] = jnp.where(lane >= NOPE, kn, kr)

    ones = jnp.ones((BF16_ROWS, TM), BF16)
    zeros = jnp.zeros((SLAB - QK_ROWS, TM), BF16)

    def score_block(pl_, qi, hh, j):
        h = 2 * pl_ + hh
        q = qT_ref[0, h * QK_ROWS:(h + 1) * QK_ROWS, qi * TM:(qi + 1) * TM]
        q_slab = jnp.concatenate([q, zeros] if hh == 0 else [q[:ROPE], zeros, q[ROPE:]], axis=0)
        return _dot(kslab_ref[h, j * TM:(j + 1) * TM, :], q_slab)

    def bias_row(pl_, qi, hh, j):
        return None

    def value_block(pl_, qi, hh, j):
        h = 2 * pl_ + hh
        return jnp.concatenate([vT_ref[0, h * V_HD:(h + 1) * V_HD, j * TM:(j + 1) * TM], ones], axis=0)

    _attend_pairs(S // TM, score_block, bias_row, value_block, gT_ref, og_ref)


def _mla_attn(qT, kn, kr, vT, gT):
    B, _, S = qT.shape
    hs = 2 * PAIRS_PER_STEP
    return pl.pallas_call(
        _mla_attn_kernel,
        grid=(B, H // hs),
        in_specs=[
            pl.BlockSpec((1, hs * QK_ROWS, S), lambda b, p: (b, p, 0)),
            pl.BlockSpec((1, S, hs * NOPE), lambda b, p: (b, 0, p)),
            pl.BlockSpec((1, S, LANES), lambda b, p: (b, 0, 0)),
            pl.BlockSpec((1, hs * V_HD, S), lambda b, p: (b, p, 0)),
            pl.BlockSpec((1, hs * V_HD, S), lambda b, p: (b, p, 0)),
        ],
        out_specs=pl.BlockSpec((1, S, hs * V_HD), lambda b, p: (b, 0, p)),
        out_shape=jax.ShapeDtypeStruct((B, S, H * V_HD), BF16),
        scratch_shapes=[pltpu.VMEM((hs, S, SLAB), BF16)],
        compiler_params=pltpu.CompilerParams(
            dimension_semantics=("arbitrary", "arbitrary"), vmem_limit_bytes=VMEM_LIMIT),
        name="mla_attn",
    )(qT, kn, kr, vT, gT)


def _out_ln_kernel(og_ref, x_ref, wo_ref, g_ref, b_ref, o_ref):
    for r in range(TM_OUT // LN_ROWS):
        rows = slice(r * LN_ROWS, (r + 1) * LN_ROWS)
        z = ALPHA * x_ref[rows, :] + _dot(og_ref[rows, :], wo_ref[...])
        mu = jnp.mean(z, axis=-1, keepdims=True)
        zc = z - mu
        var = jnp.mean(zc * zc, axis=-1, keepdims=True)
        o_ref[rows, :] = zc * lax.rsqrt(var + LN_EPS) * g_ref[...] + b_ref[...]


def _out_ln(og, x, wo, g, b):
    R, D = x.shape
    W = og.shape[1]
    return pl.pallas_call(
        _out_ln_kernel,
        grid=(R // TM_OUT,),
        in_specs=[
            pl.BlockSpec((TM_OUT, W), lambda i: (i, 0)),
            pl.BlockSpec((TM_OUT, D), lambda i: (i, 0)),
            pl.BlockSpec((W, D), lambda i: (0, 0)),
            pl.BlockSpec((1, D), lambda i: (0, 0)),
            pl.BlockSpec((1, D), lambda i: (0, 0)),
        ],
        out_specs=pl.BlockSpec((TM_OUT, D), lambda i: (i, 0)),
        out_shape=jax.ShapeDtypeStruct((R, D), F32),
        compiler_params=pltpu.CompilerParams(
            dimension_semantics=("arbitrary",), vmem_limit_bytes=VMEM_LIMIT),
        name="out_ln",
    )(og, x, wo, g, b)


def _moba_proj_kernel(x_ref, wk_ref, wvT_ref, wqT_ref, wgT_ref, ck_ref, s1_ref, s2_ref, cT_ref, sT_ref,
                      k_ref, vT_ref, qT_ref, gT_ref, sb_ref, kmt_ref):
    i = pl.program_id(1)
    nb = kmt_ref.shape[0] // H

    @pl.when(i == 0)
    def _():
        kmt_ref[...] = jnp.zeros_like(kmt_ref)

    xb = x_ref[0].astype(BF16)
    qT = _dot_nt(wqT_ref[...], xb).reshape(H, HD_B, TM)
    k = _dot(xb, wk_ref[...])
    rot = jnp.concatenate([-qT[:, HD_B // 2:], qT[:, :HD_B // 2]], axis=1)
    q_r = (qT * cT_ref[...][None] + rot * sT_ref[...][None]).reshape(H * HD_B, TM).astype(BF16)
    qT_ref[0] = q_r

    g = _dot(kmt_ref[...].astype(BF16), q_r).reshape(nb, H, TM)
    gT_ref[0] = _silu(_dot_nt(wgT_ref[...], xb)).astype(BF16)
    vT_ref[0] = _dot_nt(wvT_ref[...], xb).astype(BF16)

    parts = []
    for c in range(H * HD_B // LANES):
        kc = k[:, c * LANES:(c + 1) * LANES]
        parts.append(kc * ck_ref[...] + pltpu.roll(kc, LANES - HD_B // 2, 1) * s1_ref[...]
                     + pltpu.roll(kc, HD_B // 2, 1) * s2_ref[...])
    k_r = jnp.concatenate(parts, axis=1)
    k_ref[0] = k_r.astype(BF16)

    g_eff = [g[j] + jnp.where(j < i, 0.0, NEG) for j in range(nb)]
    rows = []
    for j in range(nb):
        rank = jnp.zeros((H, TM), F32)
        for j2 in range(nb):
            if j2 == j:
                continue
            beats = (g_eff[j2] >= g_eff[j]) if j2 < j else (g_eff[j2] > g_eff[j])
            rank = rank + jnp.where(beats, 1.0, 0.0)
        rows.append(jnp.where(rank < float(MOBA_TOPK), jnp.where(j < i, 0.0, NEG), NEG))
    sb_ref[0] = jnp.concatenate(rows, axis=0)

    k_mean = jnp.sum(k_r, axis=0, keepdims=True) * (1.0 / TM)
    head_of_lane = lax.broadcasted_iota(jnp.int32, (H, H * HD_B), 1) // HD_B
    head_of_row = lax.broadcasted_iota(jnp.int32, (H, H * HD_B), 0)
    kmt_ref[pl.ds(pl.multiple_of(i * H, H), H), :] = jnp.where(
        head_of_lane == head_of_row, jnp.broadcast_to(k_mean, (H, H * HD_B)), 0.0)


def _moba_proj(x, wk, wvT, wqT, wgT, ck, s1, s2, cT, sT):
    B, S, D = x.shape
    n = S // TM
    W = H * HD_B
    full = lambda a: pl.BlockSpec(a.shape, lambda b, i: (0,) * a.ndim)
    return pl.pallas_call(
        _moba_proj_kernel,
        grid=(B, n),
        in_specs=[
            pl.BlockSpec((1, TM, D), lambda b, i: (b, i, 0)),
            full(wk), full(wvT), full(wqT), full(wgT),
            pl.BlockSpec((TM, LANES), lambda b, i: (i, 0)),
            pl.BlockSpec((TM, LANES), lambda b, i: (i, 0)),
            pl.BlockSpec((TM, LANES), lambda b, i: (i, 0)),
            pl.BlockSpec((HD_B, TM), lambda b, i: (0, i)),
            pl.BlockSpec((HD_B, TM), lambda b, i: (0, i)),
        ],
        out_specs=[
            pl.BlockSpec((1, TM, W), lambda b, i: (b, i, 0)),
            pl.BlockSpec((1, W, TM), lambda b, i: (b, 0, i)),
            pl.BlockSpec((1, W, TM), lambda b, i: (b, 0, i)),
            pl.BlockSpec((1, W, TM), lambda b, i: (b, 0, i)),
            pl.BlockSpec((1, n * H, TM), lambda b, i: (b, 0, i)),
        ],
        out_shape=[
            jax.ShapeDtypeStruct((B, S, W), BF16),
            jax.ShapeDtypeStruct((B, W, S), BF16),
            jax.ShapeDtypeStruct((B, W, S), BF16),
            jax.ShapeDtypeStruct((B, W, S), BF16),
            jax.ShapeDtypeStruct((B, n * H, S), F32),
        ],
        scratch_shapes=[pltpu.VMEM((n * H, W), F32)],
        compiler_params=pltpu.CompilerParams(
            dimension_semantics=("arbitrary", "arbitrary"), vmem_limit_bytes=VMEM_LIMIT),
        name="moba_proj",
    )(x, wk, wvT, wqT, wgT, ck, s1, s2, cT, sT)


def _moba_attn_kernel(qT_ref, k_ref, vT_ref, gT_ref, sb_ref, og_ref):
    S = k_ref.shape[1]
    first_head = 2 * PAIRS_PER_STEP * pl.program_id(1)
    zeros = jnp.zeros((HD_B, TM), BF16)

    ones = jnp.ones((BF16_ROWS, TM), BF16)

    def score_block(pl_, qi, hh, j):
        h = 2 * pl_ + hh
        q = qT_ref[0, h * HD_B:(h + 1) * HD_B, qi * TM:(qi + 1) * TM]
        q_slab = jnp.concatenate([q, zeros] if hh == 0 else [zeros, q], axis=0)
        return _dot(k_ref[0, j * TM:(j + 1) * TM, pl_ * LANES:(pl_ + 1) * LANES], q_slab)

    def bias_row(pl_, qi, hh, j):
        head = first_head + 2 * pl_ + hh
        return sb_ref[0, pl.ds(j * H + head, 1), qi * TM:(qi + 1) * TM]

    def value_block(pl_, qi, hh, j):
        h = 2 * pl_ + hh
        return jnp.concatenate([vT_ref[0, h * HD_B:(h + 1) * HD_B, j * TM:(j + 1) * TM], ones], axis=0)

    _attend_pairs(S // TM, score_block, bias_row, value_block, gT_ref, og_ref)


def _moba_attn(qT, k, vT, gT, sb):
    B, W, S = qT.shape
    hs = 2 * PAIRS_PER_STEP
    return pl.pallas_call(
        _moba_attn_kernel,
        grid=(B, H // hs),
        in_specs=[
            pl.BlockSpec((1, hs * HD_B, S), lambda b, p: (b, p, 0)),
            pl.BlockSpec((1, S, hs * HD_B), lambda b, p: (b, 0, p)),
            pl.BlockSpec((1, hs * HD_B, S), lambda b, p: (b, p, 0)),
            pl.BlockSpec((1, hs * HD_B, S), lambda b, p: (b, p, 0)),
            pl.BlockSpec((1, sb.shape[1], S), lambda b, p: (b, 0, 0)),
        ],
        out_specs=pl.BlockSpec((1, S, hs * HD_B), lambda b, p: (b, 0, p)),
        out_shape=jax.ShapeDtypeStruct((B, S, W), BF16),
        compiler_params=pltpu.CompilerParams(
            dimension_semantics=("arbitrary", "arbitrary"), vmem_limit_bytes=VMEM_LIMIT),
        name="moba_attn",
    )(qT, k, vT, gT, sb)


def _rope_tables(seq, dim):
    inv = THETA ** (-jnp.arange(0, dim, 2, dtype=F32) / dim)
    ang = jnp.arange(seq, dtype=F32)[:, None] * inv[None, :]
    ang = jnp.concatenate([ang, ang], axis=-1)
    return jnp.cos(ang), jnp.sin(ang)


def _mla_layer(x, w_in, q_norm, kv_norm, w_uq, w_ukv, w_o, ln_g, ln_b, cos, sin):
    B, S, D = x.shape
    wa = w_in[:, :Q_LORA + KV_LORA].astype(BF16)
    w_kr = w_in[:, Q_LORA + KV_LORA:Q_LORA + KV_LORA + ROPE]
    w_g = w_in[:, Q_LORA + KV_LORA + ROPE:]
    wgkT = jnp.concatenate([w_g, w_kr], axis=1).astype(BF16).T
    wqT = w_uq.astype(BF16).T

    wkv = w_ukv.reshape(KV_LORA, H, NOPE + V_HD)
    wkn = wkv[..., :NOPE].reshape(KV_LORA, H * NOPE).astype(BF16)
    wvT = wkv[..., NOPE:].reshape(KV_LORA, H * V_HD).astype(BF16).T

    tk = jnp.concatenate([cos.T, sin.T], axis=0)
    tq = tk * MLA_SCALE

    qT, kn, kr, vT, gT = _mla_proj(x, wa, q_norm[None], kv_norm[None], wqT, wkn, wvT, wgkT, tq, tk)
    og = _mla_attn(qT, kn, kr, vT, gT)
    out = _out_ln(og.reshape(B * S, H * V_HD), x.reshape(B * S, D), w_o.astype(BF16), ln_g[None], ln_b[None])
    return out.reshape(B, S, D)


def _moba_layer(x, w_kv, w_in, w_o, ln_g, ln_b, cos, sin):
    B, S, D = x.shape
    W = H * HD_B
    half = HD_B // 2
    wk = w_kv[:, :W].astype(BF16)
    wvT = w_kv[:, W:].astype(BF16).T
    wqT = w_in[:, :W].astype(BF16).T
    wgT = w_in[:, W:].astype(BF16).T
    zeros = jnp.zeros((S, half), F32)
    ck = jnp.concatenate([cos, cos], axis=1)
    s1 = jnp.tile(jnp.concatenate([-sin[:, :half], zeros], axis=1), (1, 2))
    s2 = jnp.tile(jnp.concatenate([zeros, sin[:, half:]], axis=1), (1, 2))
    cT, sT = cos.T * MOBA_SCALE, sin.T * MOBA_SCALE
    k, vT, qT, gT, sb = _moba_proj(x, wk, wvT, wqT, wgT, ck, s1, s2, cT, sT)
    og = _moba_attn(qT, k, vT, gT, sb)
    out = _out_ln(og.reshape(B * S, W), x.reshape(B * S, D), w_o.astype(BF16), ln_g[None], ln_b[None])
    return out.reshape(B, S, D)


def kernel(x, mla_w_in, mla_q_norm, mla_kv_norm, mla_w_uq, mla_w_ukv, mla_w_o,
           moba_w_kv, moba_w_in, moba_w_o, ln_g, ln_b):
    S = x.shape[1]
    n_a = mla_w_in.shape[0]
    n_b = moba_w_in.shape[0]
    cos_a, sin_a = _rope_tables(S, ROPE)
    cos_b, sin_b = _rope_tables(S, HD_B)
    for l in range(n_a):
        x = _mla_layer(x, mla_w_in[l], mla_q_norm[l], mla_kv_norm[l], mla_w_uq[l], mla_w_ukv[l], mla_w_o[l],
                       ln_g[l], ln_b[l], cos_a, sin_a)
    for j in range(n_b):
        assert n_b == 1
        x = _moba_layer(x, moba_w_kv, moba_w_in[j], moba_w_o[j], ln_g[n_a + j], ln_b[n_a + j], cos_b, sin_b)
    return x
```

```python
import jax
import jax.numpy as jnp
from jax import lax
from jax.experimental import pallas as pl
from jax.experimental.pallas import tpu as pltpu

D_MODEL = 1024
H = 16
NOPE = 64
ROPE = 32
V_HD = 64
Q_LORA = 768
KV_LORA = 256
HD_B = 64
MOBA_BLOCK = 256
MOBA_TOPK = 3
THETA = 10000.0
LN_EPS = 1e-5
RMS_EPS = 1e-6
DEPTH = 2
ALPHA = (2 * DEPTH) ** 0.25

TM = MOBA_BLOCK
TM_OUT = 1024
LN_ROWS = 256
PAIRS_PER_STEP = 2
LOOKAHEAD = 2
LANES = 128
BF16_ROWS = 16
SLAB = LANES
QK_ROWS = NOPE + ROPE
LOG2E = 1.4426950408889634
MLA_SCALE = QK_ROWS ** -0.5 * LOG2E
MOBA_SCALE = HD_B ** -0.5 * LOG2E
NEG = -1e30
VMEM_LIMIT = 48 * 1024 * 1024

F32 = jnp.float32
BF16 = jnp.bfloat16
_NT = (((1,), (1,)), ((), ()))


def _dot(a, b):
    return jnp.dot(a, b, preferred_element_type=F32)


def _dot_nt(a, b):
    return lax.dot_general(a, b, _NT, preferred_element_type=F32)


def _rms_norm(c, gain):
    return c * lax.rsqrt(jnp.mean(c * c, axis=-1, keepdims=True) + RMS_EPS) * gain


def _silu(g):
    return g * jax.nn.sigmoid(g)


def _rot_rows(t):
    half = t.shape[-2] // 2
    return jnp.concatenate([-t[..., half:, :], t[..., :half, :]], axis=-2)


def _mla_proj_kernel(x_ref, wa_ref, qn_ref, kvn_ref, wqT_ref, wkn_ref, wvT_ref, wgkT_ref, tq_ref, tk_ref,
                     qT_ref, kn_ref, kr_ref, vT_ref, gT_ref):
    xb = x_ref[0].astype(BF16)
    c_kv = _dot(xb, wa_ref[:, Q_LORA:])
    c_q = _dot(xb, wa_ref[:, :Q_LORA])
    gk = _dot_nt(wgkT_ref[...], xb)
    gT_ref[0] = _silu(gk[:H * V_HD]).astype(BF16)
    ckv_n = _rms_norm(c_kv, kvn_ref[...]).astype(BF16)
    cq_n = _rms_norm(c_q, qn_ref[...]).astype(BF16)
    kn_ref[0] = _dot(ckv_n, wkn_ref[...]).astype(BF16)
    vT_ref[0] = _dot_nt(wvT_ref[...], ckv_n).astype(BF16)
    qT = _dot_nt(wqT_ref[...], cq_n).reshape(H // 2, 2, QK_ROWS, TM)

    kr = gk[H * V_HD:]
    kr = kr * tk_ref[:ROPE, :] + _rot_rows(kr) * tk_ref[ROPE:, :]
    kr_ref[0] = jnp.concatenate([kr] * (LANES // ROPE), axis=0).T.astype(BF16)

    nope = qT[:, :, :NOPE] * MLA_SCALE
    rope = qT[:, :, NOPE:]
    roped = rope * tq_ref[:ROPE, :] + _rot_rows(rope) * tq_ref[ROPE:, :]
    pair = jnp.concatenate([nope[:, 0], roped[:, 0], roped[:, 1], nope[:, 1]], axis=1)
    qT_ref[0] = pair.reshape(H * QK_ROWS, TM).astype(BF16)


def _mla_proj(x, wa, qn, kvn, wqT, wkn, wvT, wgkT, tq, tk):
    B, S, D = x.shape
    n = S // TM
    full = lambda a: pl.BlockSpec(a.shape, lambda b, i: (0,) * a.ndim)
    return pl.pallas_call(
        _mla_proj_kernel,
        grid=(B, n),
        in_specs=[
            pl.BlockSpec((1, TM, D), lambda b, i: (b, i, 0)),
            full(wa), full(qn), full(kvn), full(wqT), full(wkn), full(wvT), full(wgkT),
            pl.BlockSpec((tq.shape[0], TM), lambda b, i: (0, i)),
            pl.BlockSpec((tk.shape[0], TM), lambda b, i: (0, i)),
        ],
        out_specs=[
            pl.BlockSpec((1, H * QK_ROWS, TM), lambda b, i: (b, 0, i)),
            pl.BlockSpec((1, TM, H * NOPE), lambda b, i: (b, i, 0)),
            pl.BlockSpec((1, TM, LANES), lambda b, i: (b, i, 0)),
            pl.BlockSpec((1, H * V_HD, TM), lambda b, i: (b, 0, i)),
            pl.BlockSpec((1, H * V_HD, TM), lambda b, i: (b, 0, i)),
        ],
        out_shape=[
            jax.ShapeDtypeStruct((B, H * QK_ROWS, S), BF16),
            jax.ShapeDtypeStruct((B, S, H * NOPE), BF16),
            jax.ShapeDtypeStruct((B, S, LANES), BF16),
            jax.ShapeDtypeStruct((B, H * V_HD, S), BF16),
            jax.ShapeDtypeStruct((B, H * V_HD, S), BF16),
        ],
        compiler_params=pltpu.CompilerParams(
            dimension_semantics=("arbitrary", "arbitrary"), vmem_limit_bytes=VMEM_LIMIT),
        name="mla_proj",
    )(x, wa, qn, kvn, wqT, wkn, wvT, wgkT, tq, tk)


def _causal_mask():
    r = lax.broadcasted_iota(jnp.int32, (TM, TM), 0)
    c = lax.broadcasted_iota(jnp.int32, (TM, TM), 1)
    return r <= c


def _attend_pairs(n_q, score_block, bias_row, value_block, gT_ref, og_ref):
    mask = _causal_mask()
    steps = []
    for pl_ in range(PAIRS_PER_STEP):
        for qi in range(n_q):
            blocks = [qi] + list(range(qi))
            steps += [(pl_, qi, j, t == 0, t == qi) for t, j in enumerate(blocks)]

    def step_scores(step):
        pl_, qi, j, first = step[:4]
        out = []
        for hh in range(2):
            s = score_block(pl_, qi, hh, j)
            if first:
                s = jnp.where(mask, s, NEG)
            out.append((s, jnp.max(s, axis=0, keepdims=True)))
        return out

    ahead = [step_scores(st) for st in steps[:LOOKAHEAD]]
    m = [None, None]
    acc = [None, None]
    for n, (pl_, qi, j, first, last) in enumerate(steps):
        s_cur = ahead.pop(0)
        if n + LOOKAHEAD < len(steps):
            ahead.append(step_scores(steps[n + LOOKAHEAD]))
        for hh in range(2):
            s, cm = s_cur[hh]
            if first:
                m[hh] = cm
                p = jnp.exp2(s - m[hh]).astype(BF16)
                acc[hh] = _dot(value_block(pl_, qi, hh, j), p)
            else:
                b = bias_row(pl_, qi, hh, j)
                m_new = jnp.maximum(m[hh], cm if b is None else cm + b)
                p = jnp.exp2(s - (m_new if b is None else m_new - b)).astype(BF16)
                acc[hh] = acc[hh] * jnp.exp2(m[hh] - m_new) + _dot(value_block(pl_, qi, hh, j), p)
                m[hh] = m_new
        if last:
            hd = acc[0].shape[0] - BF16_ROWS
            outs = [acc[hh][:hd] / acc[hh][hd:hd + 1] for hh in range(2)]
            cols = slice(qi * TM, (qi + 1) * TM)
            feats = slice(pl_ * LANES, (pl_ + 1) * LANES)
            oT = jnp.concatenate(outs, axis=0) * gT_ref[0, feats, cols].astype(F32)
            og_ref[0, cols, feats] = oT.T.astype(BF16)


def _mla_attn_kernel(qT_ref, kn_ref, kr_ref, vT_ref, gT_ref, og_ref, kslab_ref):
    S = kn_ref.shape[1]
    lane = lax.broadcasted_iota(jnp.int32, (S, SLAB), 1)
    kr = kr_ref[0]
    for pl_ in range(PAIRS_PER_STEP):
        kn = kn_ref[0, :, pl_ * LANES:(pl_ + 1) * LANES]
        kslab_ref[2 * pl_] = jnp.where(lane < NOPE, kn, kr)
        kslab_ref[2 * pl_ + 1] = jnp.where(lane >= NOPE, kn, kr)

    ones = jnp.ones((BF16_ROWS, TM), BF16)
    zeros = jnp.zeros((SLAB - QK_ROWS, TM), BF16)

    def score_block(pl_, qi, hh, j):
        h = 2 * pl_ + hh
        q = qT_ref[0, h * QK_ROWS:(h + 1) * QK_ROWS, qi * TM:(qi + 1) * TM]
        q_slab = jnp.concatenate([q, zeros] if hh == 0 else [q[:ROPE], zeros, q[ROPE:]], axis=0)
        return _dot(kslab_ref[h, j * TM:(j + 1) * TM, :], q_slab)

    def bias_row(pl_, qi, hh, j):
        return None

    def value_block(pl_, qi, hh, j):
        h = 2 * pl_ + hh
        return jnp.concatenate([vT_ref[0, h * V_HD:(h + 1) * V_HD, j * TM:(j + 1) * TM], ones], axis=0)

    _attend_pairs(S // TM, score_block, bias_row, value_block, gT_ref, og_ref)


def _mla_attn(qT, kn, kr, vT, gT):
    B, _, S = qT.shape
    hs = 2 * PAIRS_PER_STEP
    return pl.pallas_call(
        _mla_attn_kernel,
        grid=(B, H // hs),
        in_specs=[
            pl.BlockSpec((1, hs * QK_ROWS, S), lambda b, p: (b, p, 0)),
            pl.BlockSpec((1, S, hs * NOPE), lambda b, p: (b, 0, p)),
            pl.BlockSpec((1, S, LANES), lambda b, p: (b, 0, 0)),
            pl.BlockSpec((1, hs * V_HD, S), lambda b, p: (b, p, 0)),
            pl.BlockSpec((1, hs * V_HD, S), lambda b, p: (b, p, 0)),
        ],
        out_specs=pl.BlockSpec((1, S, hs * V_HD), lambda b, p: (b, 0, p)),
        out_shape=jax.ShapeDtypeStruct((B, S, H * V_HD), BF16),
        scratch_shapes=[pltpu.VMEM((hs, S, SLAB), BF16)],
        compiler_params=pltpu.CompilerParams(
            dimension_semantics=("arbitrary", "arbitrary"), vmem_limit_bytes=VMEM_LIMIT),
        name="mla_attn",
    )(qT, kn, kr, vT, gT)


def _residual_layer_norm(og, x, wo, g, b):
    z = ALPHA * x + _dot(og, wo)
    mu = jnp.mean(z, axis=-1, keepdims=True)
    zc = z - mu
    var = jnp.mean(zc * zc, axis=-1, keepdims=True)
    return zc * lax.rsqrt(var + LN_EPS) * g + b


def _out_ln_kernel(og_ref, x_ref, wo_ref, g_ref, b_ref, o_ref):
    for r in range(TM_OUT // LN_ROWS):
        rows = slice(r * LN_ROWS, (r + 1) * LN_ROWS)
        o_ref[rows, :] = _residual_layer_norm(og_ref[rows, :], x_ref[rows, :], wo_ref[...],
                                              g_ref[...], b_ref[...])


def _out_ln(og, x, wo, g, b):
    R, D = x.shape
    W = og.shape[1]
    return pl.pallas_call(
        _out_ln_kernel,
        grid=(R // TM_OUT,),
        in_specs=[
            pl.BlockSpec((TM_OUT, W), lambda i: (i, 0)),
            pl.BlockSpec((TM_OUT, D), lambda i: (i, 0)),
            pl.BlockSpec((W, D), lambda i: (0, 0)),
            pl.BlockSpec((1, D), lambda i: (0, 0)),
            pl.BlockSpec((1, D), lambda i: (0, 0)),
        ],
        out_specs=pl.BlockSpec((TM_OUT, D), lambda i: (i, 0)),
        out_shape=jax.ShapeDtypeStruct((R, D), F32),
        compiler_params=pltpu.CompilerParams(
            dimension_semantics=("arbitrary",), vmem_limit_bytes=VMEM_LIMIT),
        name="out_ln",
    )(og, x, wo, g, b)


def _moba_proj_kernel(og_ref, xin_ref, wo_ref, lng_ref, lnb_ref, wk_ref, wvT_ref, wqT_ref, wgT_ref,
                      ck_ref, s1_ref, s2_ref, cT_ref, sT_ref,
                      x1_ref, k_ref, vT_ref, qT_ref, gT_ref, sb_ref, kmt_ref, xb_ref):
    t = pl.program_id(0)
    nb = kmt_ref.shape[0] // H
    i = lax.rem(jnp.maximum(t - 1, 0), nb)

    @pl.when(t == 0)
    def _():
        xb_ref[1] = jnp.zeros(xb_ref.shape[1:], BF16)

    @pl.when(i == 0)
    def _():
        kmt_ref[...] = jnp.zeros_like(kmt_ref)

    xb = xb_ref[lax.rem(t + 1, 2)]
    x1 = _residual_layer_norm(og_ref[0], xin_ref[0], wo_ref[...], lng_ref[...], lnb_ref[...])
    x1_ref[0] = x1
    qT = _dot_nt(wqT_ref[...], xb).reshape(H, HD_B, TM)
    k = _dot(xb, wk_ref[...])
    rot = jnp.concatenate([-qT[:, HD_B // 2:], qT[:, :HD_B // 2]], axis=1)
    q_r = (qT * cT_ref[...][None] + rot * sT_ref[...][None]).reshape(H * HD_B, TM).astype(BF16)
    qT_ref[0] = q_r

    gT_ref[0] = _silu(_dot_nt(wgT_ref[...], xb)).astype(BF16)
    g = _dot(kmt_ref[...].astype(BF16), q_r).reshape(nb, H, TM)
    vT_ref[0] = _dot_nt(wvT_ref[...], xb).astype(BF16)

    parts = []
    for c in range(H * HD_B // LANES):
        kc = k[:, c * LANES:(c + 1) * LANES]
        parts.append(kc * ck_ref[...] + pltpu.roll(kc, LANES - HD_B // 2, 1) * s1_ref[...]
                     + pltpu.roll(kc, HD_B // 2, 1) * s2_ref[...])
    k_r = jnp.concatenate(parts, axis=1)
    k_ref[0] = k_r.astype(BF16)

    g_eff = [g[j] + jnp.where(j < i, 0.0, NEG) for j in range(nb)]
    rows = []
    for j in range(nb):
        rank = jnp.zeros((H, TM), F32)
        for j2 in range(nb):
            if j2 == j:
                continue
            beats = (g_eff[j2] >= g_eff[j]) if j2 < j else (g_eff[j2] > g_eff[j])
            rank = rank + jnp.where(beats, 1.0, 0.0)
        rows.append(jnp.where(rank < float(MOBA_TOPK), jnp.where(j < i, 0.0, NEG), NEG))
    sb_ref[0] = jnp.concatenate(rows, axis=0)

    k_mean = jnp.sum(k_r, axis=0, keepdims=True) * (1.0 / TM)
    head_of_lane = lax.broadcasted_iota(jnp.int32, (H, H * HD_B), 1) // HD_B
    head_of_row = lax.broadcasted_iota(jnp.int32, (H, H * HD_B), 0)
    kmt_ref[pl.ds(pl.multiple_of(i * H, H), H), :] = jnp.where(
        head_of_lane == head_of_row, jnp.broadcast_to(k_mean, (H, H * HD_B)), 0.0)

    xb_ref[lax.rem(t, 2)] = x1.astype(BF16)


def _moba_proj(og, x, wo, lng, lnb, wk, wvT, wqT, wgT, ck, s1, s2, cT, sT):
    B, S, D = x.shape
    n = S // TM
    W = H * HD_B
    tiles = B * n
    full = lambda a: pl.BlockSpec(a.shape, lambda t: (0,) * a.ndim)
    ln_tile = lambda t: jnp.minimum(t, tiles - 1)
    pr_tile = lambda t: jnp.maximum(t - 1, 0)
    ln_rows = lambda t: (ln_tile(t) // n, ln_tile(t) % n, 0)
    pr_rows = lambda t: (pr_tile(t) // n, pr_tile(t) % n, 0)
    pr_cols = lambda t: (pr_tile(t) // n, 0, pr_tile(t) % n)
    return pl.pallas_call(
        _moba_proj_kernel,
        grid=(tiles + 1,),
        in_specs=[
            pl.BlockSpec((1, TM, og.shape[2]), ln_rows),
            pl.BlockSpec((1, TM, D), ln_rows),
            full(wo), full(lng), full(lnb),
            full(wk), full(wvT), full(wqT), full(wgT),
            pl.BlockSpec((TM, LANES), lambda t: (pr_tile(t) % n, 0)),
            pl.BlockSpec((TM, LANES), lambda t: (pr_tile(t) % n, 0)),
            pl.BlockSpec((TM, LANES), lambda t: (pr_tile(t) % n, 0)),
            pl.BlockSpec((HD_B, TM), lambda t: (0, pr_tile(t) % n)),
            pl.BlockSpec((HD_B, TM), lambda t: (0, pr_tile(t) % n)),
        ],
        out_specs=[
            pl.BlockSpec((1, TM, D), ln_rows),
            pl.BlockSpec((1, TM, W), pr_rows),
            pl.BlockSpec((1, W, TM), pr_cols),
            pl.BlockSpec((1, W, TM), pr_cols),
            pl.BlockSpec((1, W, TM), pr_cols),
            pl.BlockSpec((1, n * H, TM), pr_cols),
        ],
        out_shape=[
            jax.ShapeDtypeStruct((B, S, D), F32),
            jax.ShapeDtypeStruct((B, S, W), BF16),
            jax.ShapeDtypeStruct((B, W, S), BF16),
            jax.ShapeDtypeStruct((B, W, S), BF16),
            jax.ShapeDtypeStruct((B, W, S), BF16),
            jax.ShapeDtypeStruct((B, n * H, S), F32),
        ],
        scratch_shapes=[pltpu.VMEM((n * H, W), F32), pltpu.VMEM((2, TM, D), BF16)],
        compiler_params=pltpu.CompilerParams(
            dimension_semantics=("arbitrary",), vmem_limit_bytes=VMEM_LIMIT),
        name="moba_proj",
    )(og, x, wo, lng, lnb, wk, wvT, wqT, wgT, ck, s1, s2, cT, sT)


def _moba_attn_kernel(qT_ref, k_ref, vT_ref, gT_ref, sb_ref, og_ref):
    S = k_ref.shape[1]
    first_head = 2 * PAIRS_PER_STEP * pl.program_id(1)
    zeros = jnp.zeros((HD_B, TM), BF16)

    ones = jnp.ones((BF16_ROWS, TM), BF16)

    def score_block(pl_, qi, hh, j):
        h = 2 * pl_ + hh
        q = qT_ref[0, h * HD_B:(h + 1) * HD_B, qi * TM:(qi + 1) * TM]
        q_slab = jnp.concatenate([q, zeros] if hh == 0 else [zeros, q], axis=0)
        return _dot(k_ref[0, j * TM:(j + 1) * TM, pl_ * LANES:(pl_ + 1) * LANES], q_slab)

    def bias_row(pl_, qi, hh, j):
        head = first_head + 2 * pl_ + hh
        return sb_ref[0, pl.ds(j * H + head, 1), qi * TM:(qi + 1) * TM]

    def value_block(pl_, qi, hh, j):
        h = 2 * pl_ + hh
        return jnp.concatenate([vT_ref[0, h * HD_B:(h + 1) * HD_B, j * TM:(j + 1) * TM], ones], axis=0)

    _attend_pairs(S // TM, score_block, bias_row, value_block, gT_ref, og_ref)


def _moba_attn(qT, k, vT, gT, sb):
    B, W, S = qT.shape
    hs = 2 * PAIRS_PER_STEP
    return pl.pallas_call(
        _moba_attn_kernel,
        grid=(B, H // hs),
        in_specs=[
            pl.BlockSpec((1, hs * HD_B, S), lambda b, p: (b, p, 0)),
            pl.BlockSpec((1, S, hs * HD_B), lambda b, p: (b, 0, p)),
            pl.BlockSpec((1, hs * HD_B, S), lambda b, p: (b, p, 0)),
            pl.BlockSpec((1, hs * HD_B, S), lambda b, p: (b, p, 0)),
            pl.BlockSpec((1, sb.shape[1], S), lambda b, p: (b, 0, 0)),
        ],
        out_specs=pl.BlockSpec((1, S, hs * HD_B), lambda b, p: (b, 0, p)),
        out_shape=jax.ShapeDtypeStruct((B, S, W), BF16),
        compiler_params=pltpu.CompilerParams(
            dimension_semantics=("arbitrary", "arbitrary"), vmem_limit_bytes=VMEM_LIMIT),
        name="moba_attn",
    )(qT, k, vT, gT, sb)


def _rope_tables(seq, dim):
    inv = THETA ** (-jnp.arange(0, dim, 2, dtype=F32) / dim)
    ang = jnp.arange(seq, dtype=F32)[:, None] * inv[None, :]
    ang = jnp.concatenate([ang, ang], axis=-1)
    return jnp.cos(ang), jnp.sin(ang)


def _mla_mixer(x, w_in, q_norm, kv_norm, w_uq, w_ukv, cos, sin):
    wa =w_in[:, :Q_LORA + KV_LORA].astype(BF16)
    w_kr = w_in[:, Q_LORA + KV_LORA:Q_LORA + KV_LORA + ROPE]
    w_g = w_in[:, Q_LORA + KV_LORA + ROPE:]
    wgkT = jnp.concatenate([w_g, w_kr], axis=1).astype(BF16).T
    wqT = w_uq.astype(BF16).T

    wkv = w_ukv.reshape(KV_LORA, H, NOPE + V_HD)
    wkn = wkv[..., :NOPE].reshape(KV_LORA, H * NOPE).astype(BF16)
    wvT = wkv[..., NOPE:].reshape(KV_LORA, H * V_HD).astype(BF16).T

    tk = jnp.concatenate([cos.T, sin.T], axis=0)
    tq = tk * MLA_SCALE

    qT, kn, kr, vT, gT = _mla_proj(x, wa, q_norm[None], kv_norm[None], wqT, wkn, wvT, wgkT, tq, tk)
    return _mla_attn(qT, kn, kr, vT, gT)


def _moba_mixer(og_prev, x_prev, w_o_prev, ln_g_prev, ln_b_prev, w_kv, w_in, cos, sin):
    S = x_prev.shape[1]
    W = H * HD_B
    half = HD_B // 2
    wk = w_kv[:, :W].astype(BF16)
    wvT = w_kv[:, W:].astype(BF16).T
    wqT = w_in[:, :W].astype(BF16).T
    wgT = w_in[:, W:].astype(BF16).T
    zeros = jnp.zeros((S, half), F32)
    ck = jnp.concatenate([cos, cos], axis=1)
    s1 = jnp.tile(jnp.concatenate([-sin[:, :half], zeros], axis=1), (1, 2))
    s2 = jnp.tile(jnp.concatenate([zeros, sin[:, half:]], axis=1), (1, 2))
    cT, sT = cos.T * MOBA_SCALE, sin.T * MOBA_SCALE
    x1, k, vT, qT, gT, sb = _moba_proj(og_prev, x_prev, w_o_prev.astype(BF16), ln_g_prev[None], ln_b_prev[None],
                                       wk, wvT, wqT, wgT, ck, s1, s2, cT, sT)
    return x1, _moba_attn(qT, k, vT, gT, sb)


def kernel(x, mla_w_in, mla_q_norm, mla_kv_norm, mla_w_uq, mla_w_ukv, mla_w_o,
           moba_w_kv, moba_w_in, moba_w_o, ln_g, ln_b):
    B, S, D = x.shape
    assert mla_w_in.shape[0] == 1 and moba_w_in.shape[0] == 1 and ln_g.shape[0] == DEPTH
    cos_a, sin_a = _rope_tables(S, ROPE)
    cos_b, sin_b = _rope_tables(S, HD_B)
    og_a = _mla_mixer(x, mla_w_in[0], mla_q_norm[0], mla_kv_norm[0], mla_w_uq[0], mla_w_ukv[0], cos_a, sin_a)
    x1, og_b = _moba_mixer(og_a, x, mla_w_o[0], ln_g[0], ln_b[0], moba_w_kv, moba_w_in[0], cos_b, sin_b)
    out = _out_ln(og_b.reshape(B * S, H * HD_B), x1.reshape(B * S, D), moba_w_o[0].astype(BF16),
                  ln_g[1][None], ln_b[1][None])
    return out.reshape(B, S, D)
```

```python
import jax
import jax.numpy as jnp
from jax import lax
from jax.experimental import pallas as pl
from jax.experimental.pallas import tpu as pltpu

D_MODEL = 1024
H = 16
NOPE = 64
ROPE = 32
V_HD = 64
Q_LORA = 768
KV_LORA = 256
HD_B = 64
MOBA_BLOCK = 256
MOBA_TOPK = 3
THETA = 10000.0
LN_EPS = 1e-5
RMS_EPS = 1e-6
DEPTH = 2
ALPHA = (2 * DEPTH) ** 0.25

TM = MOBA_BLOCK
TM_A = 512
TM_OUT = 1024
LN_ROWS = 256
PAIRS_PER_STEP = 2
LOOKAHEAD = 2
LANES = 128
BF16_ROWS = 16
SLAB = LANES
QK_ROWS = NOPE + ROPE
LOG2E = 1.4426950408889634
MLA_SCALE = QK_ROWS ** -0.5 * LOG2E
MOBA_SCALE = HD_B ** -0.5 * LOG2E
NEG = -1e30
VMEM_LIMIT = 48 * 1024 * 1024

F32 = jnp.float32
BF16 = jnp.bfloat16
_NT = (((1,), (1,)), ((), ()))


def _dot(a, b):
    return jnp.dot(a, b, preferred_element_type=F32)


def _dot_nt(a, b):
    return lax.dot_general(a, b, _NT, preferred_element_type=F32)


def _rms_norm(c, gain):
    return c * lax.rsqrt(jnp.mean(c * c, axis=-1, keepdims=True) + RMS_EPS) * gain


def _silu(g):
    return g * jax.nn.sigmoid(g)


def _rot_rows(t):
    half = t.shape[-2] // 2
    return jnp.concatenate([-t[..., half:, :], t[..., :half, :]], axis=-2)


def _mla_proj_kernel(x_ref, wa_ref, qn_ref, kvn_ref, wqT_ref, wkn_ref, wvT_ref, wgkT_ref, tq_ref, tk_ref,
                     qT_ref, kn_ref, kr_ref, vT_ref, gT_ref):
    tm = x_ref.shape[1]
    xb = x_ref[0].astype(BF16)
    c_kv = _dot(xb, wa_ref[:, Q_LORA:])
    c_q = _dot(xb, wa_ref[:, :Q_LORA])
    gk = _dot_nt(wgkT_ref[...], xb)
    gT_ref[0] = _silu(gk[:H * V_HD]).astype(BF16)
    ckv_n = _rms_norm(c_kv, kvn_ref[...]).astype(BF16)
    cq_n = _rms_norm(c_q, qn_ref[...]).astype(BF16)
    kn_ref[0] = _dot(ckv_n, wkn_ref[...]).astype(BF16)
    vT_ref[0] = _dot_nt(wvT_ref[...], ckv_n).astype(BF16)
    qT = _dot_nt(wqT_ref[...], cq_n).reshape(H // 2, 2, QK_ROWS, tm)

    kr = gk[H * V_HD:]
    kr = kr * tk_ref[:ROPE, :] + _rot_rows(kr) * tk_ref[ROPE:, :]
    kr_ref[0] = jnp.concatenate([kr] * (LANES // ROPE), axis=0).T.astype(BF16)

    nope = qT[:, :, :NOPE] * MLA_SCALE
    rope = qT[:, :, NOPE:]
    roped = rope * tq_ref[:ROPE, :] + _rot_rows(rope) * tq_ref[ROPE:, :]
    pair = jnp.concatenate([nope[:, 0], roped[:, 0], roped[:, 1], nope[:, 1]], axis=1)
    qT_ref[0] = pair.reshape(H * QK_ROWS, tm).astype(BF16)


def _mla_proj(x, wa, qn, kvn, wqT, wkn, wvT, wgkT, tq, tk):
    B, S, D = x.shape
    n = S // TM_A
    full = lambda a: pl.BlockSpec(a.shape, lambda b, i: (0,) * a.ndim)
    return pl.pallas_call(
        _mla_proj_kernel,
        grid=(B, n),
        in_specs=[
            pl.BlockSpec((1, TM_A, D), lambda b, i: (b, i, 0)),
            full(wa), full(qn), full(kvn), full(wqT), full(wkn), full(wvT), full(wgkT),
            pl.BlockSpec((tq.shape[0], TM_A), lambda b, i: (0, i)),
            pl.BlockSpec((tk.shape[0], TM_A), lambda b, i: (0, i)),
        ],
        out_specs=[
            pl.BlockSpec((1, H * QK_ROWS, TM_A), lambda b, i: (b, 0, i)),
            pl.BlockSpec((1, TM_A, H * NOPE), lambda b, i: (b, i, 0)),
            pl.BlockSpec((1, TM_A, LANES), lambda b, i: (b, i, 0)),
            pl.BlockSpec((1, H * V_HD, TM_A), lambda b, i: (b, 0, i)),
            pl.BlockSpec((1, H * V_HD, TM_A), lambda b, i: (b, 0, i)),
        ],
        out_shape=[
            jax.ShapeDtypeStruct((B, H * QK_ROWS, S), BF16),
            jax.ShapeDtypeStruct((B, S, H * NOPE), BF16),
            jax.ShapeDtypeStruct((B, S, LANES), BF16),
            jax.ShapeDtypeStruct((B, H * V_HD, S), BF16),
            jax.ShapeDtypeStruct((B, H * V_HD, S), BF16),
        ],
        compiler_params=pltpu.CompilerParams(
            dimension_semantics=("arbitrary", "arbitrary"), vmem_limit_bytes=VMEM_LIMIT),
        name="mla_proj",
    )(x, wa, qn, kvn, wqT, wkn, wvT, wgkT, tq, tk)


def _causal_mask():
    r = lax.broadcasted_iota(jnp.int32, (TM, TM), 0)
    c = lax.broadcasted_iota(jnp.int32, (TM, TM), 1)
    return r <= c


def _attend_pairs(n_q, score_block, bias_row, value_block, gT_ref, og_ref):
    mask = _causal_mask()
    per_pair = []
    for pl_ in range(PAIRS_PER_STEP):
        per_pair.append([(pl_, qi, j, t == 0, t == qi)
                         for qi in range(n_q) for t, j in enumerate([qi] + list(range(qi)))])
    steps = [st for group in zip(*per_pair) for st in group]

    def step_scores(step):
        pl_, qi, j, first = step[:4]
        out = []
        for hh in range(2):
            s = score_block(pl_, qi, hh, j)
            if first:
                s = jnp.where(mask, s, NEG)
            out.append((s, jnp.max(s, axis=0, keepdims=True)))
        return out

    ahead = [step_scores(st) for st in steps[:LOOKAHEAD]]
    m = {}
    acc = {}
    for n, (pl_, qi, j, first, last) in enumerate(steps):
        s_cur = ahead.pop(0)
        if n + LOOKAHEAD < len(steps):
            ahead.append(step_scores(steps[n + LOOKAHEAD]))
        for hh in range(2):
            s, cm = s_cur[hh]
            key = (pl_, hh)
            if first:
                m[key] = cm
                p = jnp.exp2(s - cm).astype(BF16)
                acc[key] = _dot(value_block(pl_, qi, hh, j), p)
            else:
                b = bias_row(pl_, qi, hh, j)
                m_new = jnp.maximum(m[key], cm if b is None else cm + b)
                p = jnp.exp2(s - (m_new if b is None else m_new - b)).astype(BF16)
                acc[key] = acc[key] * jnp.exp2(m[key] - m_new) + _dot(value_block(pl_, qi, hh, j), p)
                m[key] = m_new
        if last:
            hd = acc[(pl_, 0)].shape[0] - BF16_ROWS
            outs = [acc[(pl_, hh)][:hd] / acc[(pl_, hh)][hd:hd + 1] for hh in range(2)]
            cols = slice(qi * TM, (qi + 1) * TM)
            feats = slice(pl_ * LANES, (pl_ + 1) * LANES)
            oT = jnp.concatenate(outs, axis=0) * gT_ref[0, feats, cols].astype(F32)
            og_ref[0, cols, feats] = oT.T.astype(BF16)


def _mla_attn_kernel(qT_ref, kn_ref, kr_ref, vT_ref, gT_ref, og_ref, kslab_ref):
    S = kn_ref.shape[1]
    lane = lax.broadcasted_iota(jnp.int32, (S, SLAB), 1)
    kr = kr_ref[0]
    for pl_ in range(PAIRS_PER_STEP):
        kn = kn_ref[0, :, pl_ * LANES:(pl_ + 1) * LANES]
        kslab_ref[2 * pl_] = jnp.where(lane < NOPE, kn, kr)
        kslab_ref[2 * pl_ + 1] = jnp.where(lane >= NOPE, kn, kr)

    ones = jnp.ones((BF16_ROWS, TM), BF16)
    zeros = jnp.zeros((SLAB - QK_ROWS, TM), BF16)

    def score_block(pl_, qi, hh, j):
        h = 2 * pl_ + hh
        q = qT_ref[0, h * QK_ROWS:(h + 1) * QK_ROWS, qi * TM:(qi + 1) * TM]
        q_slab = jnp.concatenate([q, zeros] if hh == 0 else [q[:ROPE], zeros, q[ROPE:]], axis=0)
        return _dot(kslab_ref[h, j * TM:(j + 1) * TM, :], q_slab)

    def bias_row(pl_, qi, hh, j):
        return None

    def value_block(pl_, qi, hh, j):
        h = 2 * pl_ + hh
        return jnp.concatenate([vT_ref[0, h * V_HD:(h + 1) * V_HD, j * TM:(j + 1) * TM], ones], axis=0)

    _attend_pairs(S // TM, score_block, bias_row, value_block, gT_ref, og_ref)


def _mla_attn(qT, kn, kr, vT, gT):
    B, _, S = qT.shape
    hs = 2 * PAIRS_PER_STEP
    return pl.pallas_call(
        _mla_attn_kernel,
        grid=(B, H // hs),
        in_specs=[
            pl.BlockSpec((1, hs * QK_ROWS, S), lambda b, p: (b, p, 0)),
            pl.BlockSpec((1, S, hs * NOPE), lambda b, p: (b, 0, p)),
            pl.BlockSpec((1, S, LANES), lambda b, p: (b, 0, 0)),
            pl.BlockSpec((1, hs * V_HD, S), lambda b, p: (b, p, 0)),
            pl.BlockSpec((1, hs * V_HD, S), lambda b, p: (b, p, 0)),
        ],
        out_specs=pl.BlockSpec((1, S, hs * V_HD), lambda b, p: (b, 0, p)),
        out_shape=jax.ShapeDtypeStruct((B, S, H * V_HD), BF16),
        scratch_shapes=[pltpu.VMEM((hs, S, SLAB), BF16)],
        compiler_params=pltpu.CompilerParams(
            dimension_semantics=("arbitrary", "arbitrary"), vmem_limit_bytes=VMEM_LIMIT),
        name="mla_attn",
    )(qT, kn, kr, vT, gT)


def _residual_layer_norm(og, x, wo, g, b):
    z = ALPHA * x + _dot(og, wo)
    mu = jnp.mean(z, axis=-1, keepdims=True)
    zc = z - mu
    var = jnp.mean(zc * zc, axis=-1, keepdims=True)
    return zc * lax.rsqrt(var + LN_EPS) * g + b


def _out_ln_kernel(og_ref, x_ref, wo_ref, g_ref, b_ref, o_ref):
    for r in range(TM_OUT // LN_ROWS):
        rows = slice(r * LN_ROWS, (r + 1) * LN_ROWS)
        o_ref[rows, :] = _residual_layer_norm(og_ref[rows, :], x_ref[rows, :], wo_ref[...],
                                              g_ref[...], b_ref[...])


def _out_ln(og, x, wo, g, b):
    R, D = x.shape
    W = og.shape[1]
    return pl.pallas_call(
        _out_ln_kernel,
        grid=(R // TM_OUT,),
        in_specs=[
            pl.BlockSpec((TM_OUT, W), lambda i: (i, 0)),
            pl.BlockSpec((TM_OUT, D), lambda i: (i, 0)),
            pl.BlockSpec((W, D), lambda i: (0, 0)),
            pl.BlockSpec((1, D), lambda i: (0, 0)),
            pl.BlockSpec((1, D), lambda i: (0, 0)),
        ],
        out_specs=pl.BlockSpec((TM_OUT, D), lambda i: (i, 0)),
        out_shape=jax.ShapeDtypeStruct((R, D), F32),
        compiler_params=pltpu.CompilerParams(
            dimension_semantics=("arbitrary",), vmem_limit_bytes=VMEM_LIMIT),
        name="out_ln",
    )(og, x, wo, g, b)


def _moba_proj_kernel(og_ref, xin_ref, wo_ref, lng_ref, lnb_ref, wk_ref, wvT_ref, wqT_ref, wgT_ref,
                      ck_ref, s1_ref, s2_ref, cT_ref, sT_ref,
                      x1_ref, k_ref, vT_ref, qT_ref, gT_ref, sb_ref, kmt_ref, xb_ref):
    t = pl.program_id(0)
    nb = kmt_ref.shape[0] // H
    i = lax.rem(jnp.maximum(t - 1, 0), nb)

    @pl.when(t == 0)
    def _():
        xb_ref[1] = jnp.zeros(xb_ref.shape[1:], BF16)

    @pl.when(i == 0)
    def _():
        kmt_ref[...] = jnp.zeros_like(kmt_ref)

    xb = xb_ref[lax.rem(t + 1, 2)]
    x1 = _residual_layer_norm(og_ref[0], xin_ref[0], wo_ref[...], lng_ref[...], lnb_ref[...])
    x1_ref[0] = x1
    qT = _dot_nt(wqT_ref[...], xb).reshape(H, HD_B, TM)
    k = _dot(xb, wk_ref[...])
    rot = jnp.concatenate([-qT[:, HD_B // 2:], qT[:, :HD_B // 2]], axis=1)
    q_r = (qT * cT_ref[...][None] + rot * sT_ref[...][None]).reshape(H * HD_B, TM).astype(BF16)
    qT_ref[0] = q_r

    gT_ref[0] = _silu(_dot_nt(wgT_ref[...], xb)).astype(BF16)
    g = _dot(kmt_ref[...].astype(BF16), q_r).reshape(nb, H, TM)
    vT_ref[0] = _dot_nt(wvT_ref[...], xb).astype(BF16)

    parts = []
    for c in range(H * HD_B // LANES):
        kc = k[:, c * LANES:(c + 1) * LANES]
        parts.append(kc * ck_ref[...] + pltpu.roll(kc, LANES - HD_B // 2, 1) * s1_ref[...]
                     + pltpu.roll(kc, HD_B // 2, 1) * s2_ref[...])
    k_r = jnp.concatenate(parts, axis=1)
    k_ref[0] = k_r.astype(BF16)

    g_eff = [g[j] + jnp.where(j < i, 0.0, NEG) for j in range(nb)]
    rows = []
    for j in range(nb):
        rank = jnp.zeros((H, TM), F32)
        for j2 in range(nb):
            if j2 == j:
                continue
            beats = (g_eff[j2] >= g_eff[j]) if j2 < j else (g_eff[j2] > g_eff[j])
            rank = rank + jnp.where(beats, 1.0, 0.0)
        rows.append(jnp.where(rank < float(MOBA_TOPK), jnp.where(j < i, 0.0, NEG), NEG))
    sb_ref[0] = jnp.concatenate(rows, axis=0)

    k_mean = jnp.sum(k_r, axis=0, keepdims=True) * (1.0 / TM)
    head_of_lane = lax.broadcasted_iota(jnp.int32, (H, H * HD_B), 1) // HD_B
    head_of_row = lax.broadcasted_iota(jnp.int32, (H, H * HD_B), 0)
    kmt_ref[pl.ds(pl.multiple_of(i * H, H), H), :] = jnp.where(
        head_of_lane == head_of_row, jnp.broadcast_to(k_mean, (H, H * HD_B)), 0.0)

    xb_ref[lax.rem(t, 2)] = x1.astype(BF16)


def _moba_proj(og, x, wo, lng, lnb, wk, wvT, wqT, wgT, ck, s1, s2, cT, sT):
    B, S, D = x.shape
    n = S // TM
    W = H * HD_B
    tiles = B * n
    full = lambda a: pl.BlockSpec(a.shape, lambda t: (0,) * a.ndim)
    ln_tile = lambda t: jnp.minimum(t, tiles - 1)
    pr_tile = lambda t: jnp.maximum(t - 1, 0)
    ln_rows = lambda t: (ln_tile(t) // n, ln_tile(t) % n, 0)
    pr_rows = lambda t: (pr_tile(t) // n, pr_tile(t) % n, 0)
    pr_cols = lambda t: (pr_tile(t) // n, 0, pr_tile(t) % n)
    return pl.pallas_call(
        _moba_proj_kernel,
        grid=(tiles + 1,),
        in_specs=[
            pl.BlockSpec((1, TM, og.shape[2]), ln_rows),
            pl.BlockSpec((1, TM, D), ln_rows),
            full(wo), full(lng), full(lnb),
            full(wk), full(wvT), full(wqT), full(wgT),
            pl.BlockSpec((TM, LANES), lambda t: (pr_tile(t) % n, 0)),
            pl.BlockSpec((TM, LANES), lambda t: (pr_tile(t) % n, 0)),
            pl.BlockSpec((TM, LANES), lambda t: (pr_tile(t) % n, 0)),
            pl.BlockSpec((HD_B, TM), lambda t: (0, pr_tile(t) % n)),
            pl.BlockSpec((HD_B, TM), lambda t: (0, pr_tile(t) % n)),
        ],
        out_specs=[
            pl.BlockSpec((1, TM, D), ln_rows),
            pl.BlockSpec((1, TM, W), pr_rows),
            pl.BlockSpec((1, W, TM), pr_cols),
            pl.BlockSpec((1, W, TM), pr_cols),
            pl.BlockSpec((1, W, TM), pr_cols),
            pl.BlockSpec((1, n * H, TM), pr_cols),
        ],
        out_shape=[
            jax.ShapeDtypeStruct((B, S, D), F32),
            jax.ShapeDtypeStruct((B, S, W), BF16),
            jax.ShapeDtypeStruct((B, W, S), BF16),
            jax.ShapeDtypeStruct((B, W, S), BF16),
            jax.ShapeDtypeStruct((B, W, S), BF16),
            jax.ShapeDtypeStruct((B, n * H, S), F32),
        ],
        scratch_shapes=[pltpu.VMEM((n * H, W), F32), pltpu.VMEM((2, TM, D), BF16)],
        compiler_params=pltpu.CompilerParams(
            dimension_semantics=("arbitrary",), vmem_limit_bytes=VMEM_LIMIT),
        name="moba_proj",
    )(og, x, wo, lng, lnb, wk, wvT, wqT, wgT, ck, s1, s2, cT, sT)


def _moba_attn_kernel(qT_ref, k_ref, vT_ref, gT_ref, sb_ref, og_ref):
    S = k_ref.shape[1]
    first_head = 2 * PAIRS_PER_STEP * pl.program_id(1)
    zeros = jnp.zeros((HD_B, TM), BF16)

    ones = jnp.ones((BF16_ROWS, TM), BF16)

    def score_block(pl_, qi, hh, j):
        h = 2 * pl_ + hh
        q = qT_ref[0, h * HD_B:(h + 1) * HD_B, qi * TM:(qi + 1) * TM]
        q_slab = jnp.concatenate([q, zeros] if hh == 0 else [zeros, q], axis=0)
        return _dot(k_ref[0, j * TM:(j + 1) * TM, pl_ * LANES:(pl_ + 1) * LANES], q_slab)

    def bias_row(pl_, qi, hh, j):
        head = first_head + 2 * pl_ + hh
        return sb_ref[0, pl.ds(j * H + head, 1), qi * TM:(qi + 1) * TM]

    def value_block(pl_, qi, hh, j):
        h = 2 * pl_ + hh
        return jnp.concatenate([vT_ref[0, h * HD_B:(h + 1) * HD_B, j * TM:(j + 1) * TM], ones], axis=0)

    _attend_pairs(S // TM, score_block, bias_row, value_block, gT_ref, og_ref)


def _moba_attn(qT, k, vT, gT, sb):
    B, W, S = qT.shape
    hs = 2 * PAIRS_PER_STEP
    return pl.pallas_call(
        _moba_attn_kernel,
        grid=(B, H // hs),
        in_specs=[
            pl.BlockSpec((1, hs * HD_B, S), lambda b, p: (b, p, 0)),
            pl.BlockSpec((1, S, hs * HD_B), lambda b, p: (b, 0, p)),
            pl.BlockSpec((1, hs * HD_B, S), lambda b, p: (b, p, 0)),
            pl.BlockSpec((1, hs * HD_B, S), lambda b, p: (b, p, 0)),
            pl.BlockSpec((1, sb.shape[1], S), lambda b, p: (b, 0, 0)),
        ],
        out_specs=pl.BlockSpec((1, S, hs * HD_B), lambda b, p: (b, 0, p)),
        out_shape=jax.ShapeDtypeStruct((B, S, W), BF16),
        compiler_params=pltpu.CompilerParams(
            dimension_semantics=("arbitrary", "arbitrary"), vmem_limit_bytes=VMEM_LIMIT),
        name="moba_attn",
    )(qT, k, vT, gT, sb)


def _rope_tables(seq, dim):
    inv = THETA ** (-jnp.arange(0, dim, 2, dtype=F32) / dim)
    ang = jnp.arange(seq, dtype=F32)[:, None] * inv[None, :]
    ang = jnp.concatenate([ang, ang], axis=-1)
    return jnp.cos(ang), jnp.sin(ang)


def _mla_mixer(x, w_in, q_norm, kv_norm, w_uq, w_ukv, cos, sin):
    wa =w_in[:, :Q_LORA + KV_LORA].astype(BF16)
    w_kr = w_in[:, Q_LORA + KV_LORA:Q_LORA + KV_LORA + ROPE]
    w_g = w_in[:, Q_LORA + KV_LORA + ROPE:]
    wgkT = jnp.concatenate([w_g, w_kr], axis=1).astype(BF16).T
    wqT = w_uq.astype(BF16).T

    wkv = w_ukv.reshape(KV_LORA, H, NOPE + V_HD)
    wkn = wkv[..., :NOPE].reshape(KV_LORA, H * NOPE).astype(BF16)
    wvT = wkv[..., NOPE:].reshape(KV_LORA, H * V_HD).astype(BF16).T

    tk = jnp.concatenate([cos.T, sin.T], axis=0)
    tq = tk * MLA_SCALE

    qT, kn, kr, vT, gT = _mla_proj(x, wa, q_norm[None], kv_norm[None], wqT, wkn, wvT, wgkT, tq, tk)
    return _mla_attn(qT, kn, kr, vT, gT)


def _moba_mixer(og_prev, x_prev, w_o_prev, ln_g_prev, ln_b_prev, w_kv, w_in, cos, sin):
    S = x_prev.shape[1]
    W = H * HD_B
    half = HD_B // 2
    wk = w_kv[:, :W].astype(BF16)
    wvT = w_kv[:, W:].astype(BF16).T
    wqT = w_in[:, :W].astype(BF16).T
    wgT = w_in[:, W:].astype(BF16).T
    zeros = jnp.zeros((S, half), F32)
    ck = jnp.concatenate([cos, cos], axis=1)
    s1 = jnp.tile(jnp.concatenate([-sin[:, :half], zeros], axis=1), (1, 2))
    s2 = jnp.tile(jnp.concatenate([zeros, sin[:, half:]], axis=1), (1, 2))
    cT, sT = cos.T * MOBA_SCALE, sin.T * MOBA_SCALE
    x1, k, vT, qT, gT, sb = _moba_proj(og_prev, x_prev, w_o_prev.astype(BF16), ln_g_prev[None], ln_b_prev[None],
                                       wk, wvT, wqT, wgT, ck, s1, s2, cT, sT)
    return x1, _moba_attn(qT, k, vT, gT, sb)


def kernel(x, mla_w_in, mla_q_norm, mla_kv_norm, mla_w_uq, mla_w_ukv, mla_w_o,
           moba_w_kv, moba_w_in, moba_w_o, ln_g, ln_b):
    B, S, D = x.shape
    assert mla_w_in.shape[0] == 1 and moba_w_in.shape[0] == 1 and ln_g.shape[0] == DEPTH
    cos_a, sin_a = _rope_tables(S, ROPE)
    cos_b, sin_b = _rope_tables(S, HD_B)
    og_a = _mla_mixer(x, mla_w_in[0], mla_q_norm[0], mla_kv_norm[0], mla_w_uq[0], mla_w_ukv[0], cos_a, sin_a)
    x1, og_b = _moba_mixer(og_a, x, mla_w_o[0], ln_g[0], ln_b[0], moba_w_kv, moba_w_in[0], cos_b, sin_b)
    out = _out_ln(og_b.reshape(B * S, H * HD_B), x1.reshape(B * S, D), moba_w_o[0].astype(BF16),
                  ln_g[1][None], ln_b[1][None])
    return out.reshape(B, S, D)
```

```python
import jax
import jax.numpy as jnp
from jax import lax
from jax.experimental import pallas as pl
from jax.experimental.pallas import tpu as pltpu

D_MODEL = 1024
H = 16
NOPE = 64
ROPE = 32
V_HD = 64
Q_LORA = 768
KV_LORA = 256
HD_B = 64
MOBA_BLOCK = 256
MOBA_TOPK = 3
THETA = 10000.0
LN_EPS = 1e-5
RMS_EPS = 1e-6
DEPTH = 2
ALPHA = (2 * DEPTH) ** 0.25

TM = MOBA_BLOCK
TM_A = 512
TM_OUT = 1024
LN_ROWS = 256
PAIRS_PER_STEP = 4
LOOKAHEAD = 2
LANES = 128
BF16_ROWS = 16
SLAB = LANES
QK_ROWS = NOPE + ROPE
LOG2E = 1.4426950408889634
MLA_SCALE = QK_ROWS ** -0.5 * LOG2E
MOBA_SCALE = HD_B ** -0.5 * LOG2E
NEG = -1e30
VMEM_LIMIT = 48 * 1024 * 1024

F32 = jnp.float32
BF16 = jnp.bfloat16
_NT = (((1,), (1,)), ((), ()))


def _dot(a, b):
    return jnp.dot(a, b, preferred_element_type=F32)


def _dot_nt(a, b):
    return lax.dot_general(a, b, _NT, preferred_element_type=F32)


def _rms_norm(c, gain):
    return c * lax.rsqrt(jnp.mean(c * c, axis=-1, keepdims=True) + RMS_EPS) * gain


def _silu(g):
    return g * jax.nn.sigmoid(g)


def _rot_rows(t):
    half = t.shape[-2] // 2
    return jnp.concatenate([-t[..., half:, :], t[..., :half, :]], axis=-2)


def _mla_proj_kernel(x_ref, wa_ref, qn_ref, kvn_ref, wqT_ref, wkn_ref, wvT_ref, wgkT_ref, tq_ref, tk_ref,
                     qT_ref, kn_ref, kr_ref, vT_ref, gT_ref):
    tm = x_ref.shape[1]
    xb = x_ref[0].astype(BF16)
    c_kv = _dot(xb, wa_ref[:, Q_LORA:])
    c_q = _dot(xb, wa_ref[:, :Q_LORA])
    gk = _dot_nt(wgkT_ref[...], xb)
    gT_ref[0] = _silu(gk[:H * V_HD]).astype(BF16)
    ckv_n = _rms_norm(c_kv, kvn_ref[...]).astype(BF16)
    cq_n = _rms_norm(c_q, qn_ref[...]).astype(BF16)
    kn_ref[0] = _dot(ckv_n, wkn_ref[...]).astype(BF16)
    vT_ref[0] = _dot_nt(wvT_ref[...], ckv_n).astype(BF16)
    qT = _dot_nt(wqT_ref[...], cq_n).reshape(H // 2, 2, QK_ROWS, tm)

    kr = gk[H * V_HD:]
    kr = kr * tk_ref[:ROPE, :] + _rot_rows(kr) * tk_ref[ROPE:, :]
    kr_ref[0] = jnp.concatenate([kr] * (LANES // ROPE), axis=0).T.astype(BF16)

    nope = qT[:, :, :NOPE] * MLA_SCALE
    rope = qT[:, :, NOPE:]
    roped = rope * tq_ref[:ROPE, :] + _rot_rows(rope) * tq_ref[ROPE:, :]
    pair = jnp.concatenate([nope[:, 0], roped[:, 0], roped[:, 1], nope[:, 1]], axis=1)
    qT_ref[0] = pair.reshape(H * QK_ROWS, tm).astype(BF16)


def _mla_proj(x, wa, qn, kvn, wqT, wkn, wvT, wgkT, tq, tk):
    B, S, D = x.shape
    n = S // TM_A
    full = lambda a: pl.BlockSpec(a.shape, lambda b, i: (0,) * a.ndim)
    return pl.pallas_call(
        _mla_proj_kernel,
        grid=(B, n),
        in_specs=[
            pl.BlockSpec((1, TM_A, D), lambda b, i: (b, i, 0)),
            full(wa), full(qn), full(kvn), full(wqT), full(wkn), full(wvT), full(wgkT),
            pl.BlockSpec((tq.shape[0], TM_A), lambda b, i: (0, i)),
            pl.BlockSpec((tk.shape[0], TM_A), lambda b, i: (0, i)),
        ],
        out_specs=[
            pl.BlockSpec((1, H * QK_ROWS, TM_A), lambda b, i: (b, 0, i)),
            pl.BlockSpec((1, TM_A, H * NOPE), lambda b, i: (b, i, 0)),
            pl.BlockSpec((1, TM_A, LANES), lambda b, i: (b, i, 0)),
            pl.BlockSpec((1, H * V_HD, TM_A), lambda b, i: (b, 0, i)),
            pl.BlockSpec((1, H * V_HD, TM_A), lambda b, i: (b, 0, i)),
        ],
        out_shape=[
            jax.ShapeDtypeStruct((B, H * QK_ROWS, S), BF16),
            jax.ShapeDtypeStruct((B, S, H * NOPE), BF16),
            jax.ShapeDtypeStruct((B, S, LANES), BF16),
            jax.ShapeDtypeStruct((B, H * V_HD, S), BF16),
            jax.ShapeDtypeStruct((B, H * V_HD, S), BF16),
        ],
        compiler_params=pltpu.CompilerParams(
            dimension_semantics=("arbitrary", "arbitrary"), vmem_limit_bytes=VMEM_LIMIT),
        name="mla_proj",
    )(x, wa, qn, kvn, wqT, wkn, wvT, wgkT, tq, tk)


def _causal_mask():
    r = lax.broadcasted_iota(jnp.int32, (TM, TM), 0)
    c = lax.broadcasted_iota(jnp.int32, (TM, TM), 1)
    return r <= c


def _attend_pairs(n_q, score_block, bias_row, value_block, gT_ref, og_ref):
    mask = _causal_mask()
    per_pair = []
    for pl_ in range(PAIRS_PER_STEP):
        per_pair.append([(pl_, qi, j, t == 0, t == qi)
                         for qi in range(n_q) for t, j in enumerate([qi] + list(range(qi)))])
    steps = [st for group in zip(*per_pair) for st in group]

    def step_scores(step):
        pl_, qi, j, first = step[:4]
        out = []
        for hh in range(2):
            s = score_block(pl_, qi, hh, j)
            if first:
                s = jnp.where(mask, s, NEG)
            out.append((s, jnp.max(s, axis=0, keepdims=True)))
        return out

    ahead = [step_scores(st) for st in steps[:LOOKAHEAD]]
    m = {}
    acc = {}
    for n, (pl_, qi, j, first, last) in enumerate(steps):
        s_cur = ahead.pop(0)
        if n + LOOKAHEAD < len(steps):
            ahead.append(step_scores(steps[n + LOOKAHEAD]))
        for hh in range(2):
            s, cm = s_cur[hh]
            key = (pl_, hh)
            if first:
                m[key] = cm
                p = jnp.exp2(s - cm).astype(BF16)
                acc[key] = _dot(value_block(pl_, qi, hh, j), p)
            else:
                b = bias_row(pl_, qi, hh, j)
                m_new = jnp.maximum(m[key], cm if b is None else cm + b)
                p = jnp.exp2(s - (m_new if b is None else m_new - b)).astype(BF16)
                acc[key] = acc[key] * jnp.exp2(m[key] - m_new) + _dot(value_block(pl_, qi, hh, j), p)
                m[key] = m_new
        if last:
            hd = acc[(pl_, 0)].shape[0] - BF16_ROWS
            outs = [acc[(pl_, hh)][:hd] / acc[(pl_, hh)][hd:hd + 1] for hh in range(2)]
            cols = slice(qi * TM, (qi + 1) * TM)
            feats = slice(pl_ * LANES, (pl_ + 1) * LANES)
            oT = jnp.concatenate(outs, axis=0) * gT_ref[0, feats, cols].astype(F32)
            og_ref[0, cols, feats] = oT.T.astype(BF16)


def _mla_attn_kernel(qT_ref, kn_ref, kr_ref, vT_ref, gT_ref, og_ref, kslab_ref):
    S = kn_ref.shape[1]
    lane = lax.broadcasted_iota(jnp.int32, (S, SLAB), 1)
    kr = kr_ref[0]
    for pl_ in range(PAIRS_PER_STEP):
        kn = kn_ref[0, :, pl_ * LANES:(pl_ + 1) * LANES]
        kslab_ref[2 * pl_] = jnp.where(lane < NOPE, kn, kr)
        kslab_ref[2 * pl_ + 1] = jnp.where(lane >= NOPE, kn, kr)

    ones = jnp.ones((BF16_ROWS, TM), BF16)
    zeros = jnp.zeros((SLAB - QK_ROWS, TM), BF16)

    def score_block(pl_, qi, hh, j):
        h = 2 * pl_ + hh
        q = qT_ref[0, h * QK_ROWS:(h + 1) * QK_ROWS, qi * TM:(qi + 1) * TM]
        q_slab = jnp.concatenate([q, zeros] if hh == 0 else [q[:ROPE], zeros, q[ROPE:]], axis=0)
        return _dot(kslab_ref[h, j * TM:(j + 1) * TM, :], q_slab)

    def bias_row(pl_, qi, hh, j):
        return None

    def value_block(pl_, qi, hh, j):
        h = 2 * pl_ + hh
        return jnp.concatenate([vT_ref[0, h * V_HD:(h + 1) * V_HD, j * TM:(j + 1) * TM], ones], axis=0)

    _attend_pairs(S // TM, score_block, bias_row, value_block, gT_ref, og_ref)


def _mla_attn(qT, kn, kr, vT, gT):
    B, _, S = qT.shape
    hs = 2 * PAIRS_PER_STEP
    return pl.pallas_call(
        _mla_attn_kernel,
        grid=(B, H // hs),
        in_specs=[
            pl.BlockSpec((1, hs * QK_ROWS, S), lambda b, p: (b, p, 0)),
            pl.BlockSpec((1, S, hs * NOPE), lambda b, p: (b, 0, p)),
            pl.BlockSpec((1, S, LANES), lambda b, p: (b, 0, 0)),
            pl.BlockSpec((1, hs * V_HD, S), lambda b, p: (b, p, 0)),
            pl.BlockSpec((1, hs * V_HD, S), lambda b, p: (b, p, 0)),
        ],
        out_specs=pl.BlockSpec((1, S, hs * V_HD), lambda b, p: (b, 0, p)),
        out_shape=jax.ShapeDtypeStruct((B, S, H * V_HD), BF16),
        scratch_shapes=[pltpu.VMEM((hs, S, SLAB), BF16)],
        compiler_params=pltpu.CompilerParams(
            dimension_semantics=("arbitrary", "arbitrary"), vmem_limit_bytes=VMEM_LIMIT),
        name="mla_attn",
    )(qT, kn, kr, vT, gT)


def _residual_layer_norm(og, x, wo, g, b):
    z = ALPHA * x + _dot(og, wo)
    mu = jnp.mean(z, axis=-1, keepdims=True)
    zc = z - mu
    var = jnp.mean(zc * zc, axis=-1, keepdims=True)
    return zc * lax.rsqrt(var + LN_EPS) * g + b


def _out_ln_kernel(og_ref, x_ref, wo_ref, g_ref, b_ref, o_ref):
    for r in range(TM_OUT // LN_ROWS):
        rows = slice(r * LN_ROWS, (r + 1) * LN_ROWS)
        o_ref[rows, :] = _residual_layer_norm(og_ref[rows, :], x_ref[rows, :], wo_ref[...],
                                              g_ref[...], b_ref[...])


def _out_ln(og, x, wo, g, b):
    R, D = x.shape
    W = og.shape[1]
    return pl.pallas_call(
        _out_ln_kernel,
        grid=(R // TM_OUT,),
        in_specs=[
            pl.BlockSpec((TM_OUT, W), lambda i: (i, 0)),
            pl.BlockSpec((TM_OUT, D), lambda i: (i, 0)),
            pl.BlockSpec((W, D), lambda i: (0, 0)),
            pl.BlockSpec((1, D), lambda i: (0, 0)),
            pl.BlockSpec((1, D), lambda i: (0, 0)),
        ],
        out_specs=pl.BlockSpec((TM_OUT, D), lambda i: (i, 0)),
        out_shape=jax.ShapeDtypeStruct((R, D), F32),
        compiler_params=pltpu.CompilerParams(
            dimension_semantics=("arbitrary",), vmem_limit_bytes=VMEM_LIMIT),
        name="out_ln",
    )(og, x, wo, g, b)


def _moba_proj_kernel(og_ref, xin_ref, wo_ref, lng_ref, lnb_ref, wk_ref, wvT_ref, wqT_ref, wgT_ref,
                      ck_ref, s1_ref, s2_ref, cT_ref, sT_ref,
                      x1_ref, k_ref, vT_ref, qT_ref, gT_ref, sb_ref, kmt_ref, xb_ref):
    t = pl.program_id(0)
    nb = kmt_ref.shape[0] // H
    i = lax.rem(jnp.maximum(t - 1, 0), nb)

    @pl.when(t == 0)
    def _():
        xb_ref[1] = jnp.zeros(xb_ref.shape[1:], BF16)

    @pl.when(i == 0)
    def _():
        kmt_ref[...] = jnp.zeros_like(kmt_ref)

    xb = xb_ref[lax.rem(t + 1, 2)]
    x1 = _residual_layer_norm(og_ref[0], xin_ref[0], wo_ref[...], lng_ref[...], lnb_ref[...])
    x1_ref[0] = x1
    qT = _dot_nt(wqT_ref[...], xb).reshape(H, HD_B, TM)
    k = _dot(xb, wk_ref[...])
    rot = jnp.concatenate([-qT[:, HD_B // 2:], qT[:, :HD_B // 2]], axis=1)
    q_r = (qT * cT_ref[...][None] + rot * sT_ref[...][None]).reshape(H * HD_B, TM).astype(BF16)
    qT_ref[0] = q_r

    gT_ref[0] = _silu(_dot_nt(wgT_ref[...], xb)).astype(BF16)
    g = _dot(kmt_ref[...].astype(BF16), q_r).reshape(nb, H, TM)
    vT_ref[0] = _dot_nt(wvT_ref[...], xb).astype(BF16)

    parts = []
    for c in range(H * HD_B // LANES):
        kc = k[:, c * LANES:(c + 1) * LANES]
        parts.append(kc * ck_ref[...] + pltpu.roll(kc, LANES - HD_B // 2, 1) * s1_ref[...]
                     + pltpu.roll(kc, HD_B // 2, 1) * s2_ref[...])
    k_r = jnp.concatenate(parts, axis=1)
    k_ref[0] = k_r.astype(BF16)

    g_eff = [g[j] + jnp.where(j < i, 0.0, NEG) for j in range(nb)]
    rows = []
    for j in range(nb):
        rank = jnp.zeros((H, TM), F32)
        for j2 in range(nb):
            if j2 == j:
                continue
            beats = (g_eff[j2] >= g_eff[j]) if j2 < j else (g_eff[j2] > g_eff[j])
            rank = rank + jnp.where(beats, 1.0, 0.0)
        rows.append(jnp.where(rank < float(MOBA_TOPK), jnp.where(j < i, 0.0, NEG), NEG))
    sb_ref[0] = jnp.concatenate(rows, axis=0)

    k_mean = jnp.sum(k_r, axis=0, keepdims=True) * (1.0 / TM)
    head_of_lane = lax.broadcasted_iota(jnp.int32, (H, H * HD_B), 1) // HD_B
    head_of_row = lax.broadcasted_iota(jnp.int32, (H, H * HD_B), 0)
    kmt_ref[pl.ds(pl.multiple_of(i * H, H), H), :] = jnp.where(
        head_of_lane == head_of_row, jnp.broadcast_to(k_mean, (H, H * HD_B)), 0.0)

    xb_ref[lax.rem(t, 2)] = x1.astype(BF16)


def _moba_proj(og, x, wo, lng, lnb, wk, wvT, wqT, wgT, ck, s1, s2, cT, sT):
    B, S, D = x.shape
    n = S // TM
    W = H * HD_B
    tiles = B * n
    full = lambda a: pl.BlockSpec(a.shape, lambda t: (0,) * a.ndim)
    ln_tile = lambda t: jnp.minimum(t, tiles - 1)
    pr_tile = lambda t: jnp.maximum(t - 1, 0)
    ln_rows = lambda t: (ln_tile(t) // n, ln_tile(t) % n, 0)
    pr_rows = lambda t: (pr_tile(t) // n, pr_tile(t) % n, 0)
    pr_cols = lambda t: (pr_tile(t) // n, 0, pr_tile(t) % n)
    return pl.pallas_call(
        _moba_proj_kernel,
        grid=(tiles + 1,),
        in_specs=[
            pl.BlockSpec((1, TM, og.shape[2]), ln_rows),
            pl.BlockSpec((1, TM, D), ln_rows),
            full(wo), full(lng), full(lnb),
            full(wk), full(wvT), full(wqT), full(wgT),
            pl.BlockSpec((TM, LANES), lambda t: (pr_tile(t) % n, 0)),
            pl.BlockSpec((TM, LANES), lambda t: (pr_tile(t) % n, 0)),
            pl.BlockSpec((TM, LANES), lambda t: (pr_tile(t) % n, 0)),
            pl.BlockSpec((HD_B, TM), lambda t: (0, pr_tile(t) % n)),
            pl.BlockSpec((HD_B, TM), lambda t: (0, pr_tile(t) % n)),
        ],
        out_specs=[
            pl.BlockSpec((1, TM, D), ln_rows),
            pl.BlockSpec((1, TM, W), pr_rows),
            pl.BlockSpec((1, W, TM), pr_cols),
            pl.BlockSpec((1, W, TM), pr_cols),
            pl.BlockSpec((1, W, TM), pr_cols),
            pl.BlockSpec((1, n * H, TM), pr_cols),
        ],
        out_shape=[
            jax.ShapeDtypeStruct((B, S, D), F32),
            jax.ShapeDtypeStruct((B, S, W), BF16),
            jax.ShapeDtypeStruct((B, W, S), BF16),
            jax.ShapeDtypeStruct((B, W, S), BF16),
            jax.ShapeDtypeStruct((B, W, S), BF16),
            jax.ShapeDtypeStruct((B, n * H, S), F32),
        ],
        scratch_shapes=[pltpu.VMEM((n * H, W), F32), pltpu.VMEM((2, TM, D), BF16)],
        compiler_params=pltpu.CompilerParams(
            dimension_semantics=("arbitrary",), vmem_limit_bytes=VMEM_LIMIT),
        name="moba_proj",
    )(og, x, wo, lng, lnb, wk, wvT, wqT, wgT, ck, s1, s2, cT, sT)


def _moba_attn_kernel(qT_ref, k_ref, vT_ref, gT_ref, sb_ref, og_ref):
    S = k_ref.shape[1]
    first_head = 2 * PAIRS_PER_STEP * pl.program_id(1)
    zeros = jnp.zeros((HD_B, TM), BF16)

    ones = jnp.ones((BF16_ROWS, TM), BF16)

    def score_block(pl_, qi, hh, j):
        h = 2 * pl_ + hh
        q = qT_ref[0, h * HD_B:(h + 1) * HD_B, qi * TM:(qi + 1) * TM]
        q_slab = jnp.concatenate([q, zeros] if hh == 0 else [zeros, q], axis=0)
        return _dot(k_ref[0, j * TM:(j + 1) * TM, pl_ * LANES:(pl_ + 1) * LANES], q_slab)

    def bias_row(pl_, qi, hh, j):
        head = first_head + 2 * pl_ + hh
        return sb_ref[0, pl.ds(j * H + head, 1), qi * TM:(qi + 1) * TM]

    def value_block(pl_, qi, hh, j):
        h = 2 * pl_ + hh
        return jnp.concatenate([vT_ref[0, h * HD_B:(h + 1) * HD_B, j * TM:(j + 1) * TM], ones], axis=0)

    _attend_pairs(S // TM, score_block, bias_row, value_block, gT_ref, og_ref)


def _moba_attn(qT, k, vT, gT, sb):
    B, W, S = qT.shape
    hs = 2 * PAIRS_PER_STEP
    return pl.pallas_call(
        _moba_attn_kernel,
        grid=(B, H // hs),
        in_specs=[
            pl.BlockSpec((1, hs * HD_B, S), lambda b, p: (b, p, 0)),
            pl.BlockSpec((1, S, hs * HD_B), lambda b, p: (b, 0, p)),
            pl.BlockSpec((1, hs * HD_B, S), lambda b, p: (b, p, 0)),
            pl.BlockSpec((1, hs * HD_B, S), lambda b, p: (b, p, 0)),
            pl.BlockSpec((1, sb.shape[1], S), lambda b, p: (b, 0, 0)),
        ],
        out_specs=pl.BlockSpec((1, S, hs * HD_B), lambda b, p: (b, 0, p)),
        out_shape=jax.ShapeDtypeStruct((B, S, W), BF16),
        compiler_params=pltpu.CompilerParams(
            dimension_semantics=("arbitrary", "arbitrary"), vmem_limit_bytes=VMEM_LIMIT),
        name="moba_attn",
    )(qT, k, vT, gT, sb)


def _rope_tables(seq, dim):
    inv = THETA ** (-jnp.arange(0, dim, 2, dtype=F32) / dim)
    ang = jnp.arange(seq, dtype=F32)[:, None] * inv[None, :]
    ang = jnp.concatenate([ang, ang], axis=-1)
    return jnp.cos(ang), jnp.sin(ang)


def _mla_mixer(x, w_in, q_norm, kv_norm, w_uq, w_ukv, cos, sin):
    wa =w_in[:, :Q_LORA + KV_LORA].astype(BF16)
    w_kr = w_in[:, Q_LORA + KV_LORA:Q_LORA + KV_LORA + ROPE]
    w_g = w_in[:, Q_LORA + KV_LORA + ROPE:]
    wgkT = jnp.concatenate([w_g, w_kr], axis=1).astype(BF16).T
    wqT = w_uq.astype(BF16).T

    wkv = w_ukv.reshape(KV_LORA, H, NOPE + V_HD)
    wkn = wkv[..., :NOPE].reshape(KV_LORA, H * NOPE).astype(BF16)
    wvT = wkv[..., NOPE:].reshape(KV_LORA, H * V_HD).astype(BF16).T

    tk = jnp.concatenate([cos.T, sin.T], axis=0)
    tq = tk * MLA_SCALE

    qT, kn, kr, vT, gT = _mla_proj(x, wa, q_norm[None], kv_norm[None], wqT, wkn, wvT, wgkT, tq, tk)
    return _mla_attn(qT, kn, kr, vT, gT)


def _moba_mixer(og_prev, x_prev, w_o_prev, ln_g_prev, ln_b_prev, w_kv, w_in, cos, sin):
    S = x_prev.shape[1]
    W = H * HD_B
    half = HD_B // 2
    wk = w_kv[:, :W].astype(BF16)
    wvT = w_kv[:, W:].astype(BF16).T
    wqT = w_in[:, :W].astype(BF16).T
    wgT = w_in[:, W:].astype(BF16).T
    zeros = jnp.zeros((S, half), F32)
    ck = jnp.concatenate([cos, cos], axis=1)
    s1 = jnp.tile(jnp.concatenate([-sin[:, :half], zeros], axis=1), (1, 2))
    s2 = jnp.tile(jnp.concatenate([zeros, sin[:, half:]], axis=1), (1, 2))
    cT, sT = cos.T * MOBA_SCALE, sin.T * MOBA_SCALE
    x1, k, vT, qT, gT, sb = _moba_proj(og_prev, x_prev, w_o_prev.astype(BF16), ln_g_prev[None], ln_b_prev[None],
                                       wk, wvT, wqT, wgT, ck, s1, s2, cT, sT)
    return x1, _moba_attn(qT, k, vT, gT, sb)


def kernel(x, mla_w_in, mla_q_norm, mla_kv_norm, mla_w_uq, mla_w_ukv, mla_w_o,
           moba_w_kv, moba_w_in, moba_w_o, ln_g, ln_b):
    B, S, D = x.shape
    assert mla_w_in.shape[0] == 1 and moba_w_in.shape[0] == 1 and ln_g.shape[0] == DEPTH
    cos_a, sin_a = _rope_tables(S, ROPE)
    cos_b, sin_b = _rope_tables(S, HD_B)
    og_a = _mla_mixer(x, mla_w_in[0], mla_q_norm[0], mla_kv_norm[0], mla_w_uq[0], mla_w_ukv[0], cos_a, sin_a)
    x1, og_b = _moba_mixer(og_a, x, mla_w_o[0], ln_g[0], ln_b[0], moba_w_kv, moba_w_in[0], cos_b, sin_b)
    out = _out_ln(og_b.reshape(B * S, H * HD_B), x1.reshape(B * S, D), moba_w_o[0].astype(BF16),
                  ln_g[1][None], ln_b[1][None])
    return out.reshape(B, S, D)
```

```python
import jax
import jax.numpy as jnp
from jax import lax
from jax.experimental import pallas as pl
from jax.experimental.pallas import tpu as pltpu

D_MODEL = 1024
H = 16
NOPE = 64
ROPE = 32
V_HD = 64
Q_LORA = 768
KV_LORA = 256
HD_B = 64
MOBA_BLOCK = 256
MOBA_TOPK = 3
THETA = 10000.0
LN_EPS = 1e-5
RMS_EPS = 1e-6
DEPTH = 2
ALPHA = (2 * DEPTH) ** 0.25

TM = MOBA_BLOCK
TM_A = 512
TM_OUT = 1024
LN_ROWS = 256
RING_SLOTS = 3
PAIRS_PER_STEP = 2
LOOKAHEAD = 2
LANES = 128
BF16_ROWS = 16
SLAB = LANES
QK_ROWS = NOPE + ROPE
LOG2E = 1.4426950408889634
MLA_SCALE = QK_ROWS ** -0.5 * LOG2E
MOBA_SCALE = HD_B ** -0.5 * LOG2E
NEG = -1e30
VMEM_LIMIT = 48 * 1024 * 1024

F32 = jnp.float32
BF16 = jnp.bfloat16
_NT = (((1,), (1,)), ((), ()))


def _dot(a, b):
    return jnp.dot(a, b, preferred_element_type=F32)


def _dot_nt(a, b):
    return lax.dot_general(a, b, _NT, preferred_element_type=F32)


def _rms_norm(c, gain):
    return c * lax.rsqrt(jnp.mean(c * c, axis=-1, keepdims=True) + RMS_EPS) * gain


def _silu(g):
    return g * jax.nn.sigmoid(g)


def _rot_rows(t):
    half = t.shape[-2] // 2
    return jnp.concatenate([-t[..., half:, :], t[..., :half, :]], axis=-2)


def _mla_proj_kernel(x_ref, wa_ref, qn_ref, kvn_ref, wqT_ref, wkn_ref, wvT_ref, wgkT_ref, tq_ref, tk_ref,
                     qT_ref, kn_ref, kr_ref, vT_ref, gT_ref):
    tm = x_ref.shape[1]
    xb = x_ref[0].astype(BF16)
    c_kv = _dot(xb, wa_ref[:, Q_LORA:])
    c_q = _dot(xb, wa_ref[:, :Q_LORA])
    gk = _dot_nt(wgkT_ref[...], xb)
    gT_ref[0] = _silu(gk[:H * V_HD]).astype(BF16)
    ckv_n = _rms_norm(c_kv, kvn_ref[...]).astype(BF16)
    cq_n = _rms_norm(c_q, qn_ref[...]).astype(BF16)
    kn_ref[0] = _dot(ckv_n, wkn_ref[...]).astype(BF16)
    vT_ref[0] = _dot_nt(wvT_ref[...], ckv_n).astype(BF16)
    qT = _dot_nt(wqT_ref[...], cq_n).reshape(H // 2, 2, QK_ROWS, tm)

    kr = gk[H * V_HD:]
    kr = kr * tk_ref[:ROPE, :] + _rot_rows(kr) * tk_ref[ROPE:, :]
    kr_ref[0] = jnp.concatenate([kr] * (LANES // ROPE), axis=0).T.astype(BF16)

    nope = qT[:, :, :NOPE] * MLA_SCALE
    rope = qT[:, :, NOPE:]
    roped = rope * tq_ref[:ROPE, :] + _rot_rows(rope) * tq_ref[ROPE:, :]
    pair = jnp.concatenate([nope[:, 0], roped[:, 0], roped[:, 1], nope[:, 1]], axis=1)
    qT_ref[0] = pair.reshape(H * QK_ROWS, tm).astype(BF16)


def _mla_proj(x, wa, qn, kvn, wqT, wkn, wvT, wgkT, tq, tk):
    B, S, D = x.shape
    n = S // TM_A
    full = lambda a: pl.BlockSpec(a.shape, lambda b, i: (0,) * a.ndim)
    return pl.pallas_call(
        _mla_proj_kernel,
        grid=(B, n),
        in_specs=[
            pl.BlockSpec((1, TM_A, D), lambda b, i: (b, i, 0)),
            full(wa), full(qn), full(kvn), full(wqT), full(wkn), full(wvT), full(wgkT),
            pl.BlockSpec((tq.shape[0], TM_A), lambda b, i: (0, i)),
            pl.BlockSpec((tk.shape[0], TM_A), lambda b, i: (0, i)),
        ],
        out_specs=[
            pl.BlockSpec((1, H * QK_ROWS, TM_A), lambda b, i: (b, 0, i)),
            pl.BlockSpec((1, TM_A, H * NOPE), lambda b, i: (b, i, 0)),
            pl.BlockSpec((1, TM_A, LANES), lambda b, i: (b, i, 0)),
            pl.BlockSpec((1, H * V_HD, TM_A), lambda b, i: (b, 0, i)),
            pl.BlockSpec((1, H * V_HD, TM_A), lambda b, i: (b, 0, i)),
        ],
        out_shape=[
            jax.ShapeDtypeStruct((B, H * QK_ROWS, S), BF16),
            jax.ShapeDtypeStruct((B, S, H * NOPE), BF16),
            jax.ShapeDtypeStruct((B, S, LANES), BF16),
            jax.ShapeDtypeStruct((B, H * V_HD, S), BF16),
            jax.ShapeDtypeStruct((B, H * V_HD, S), BF16),
        ],
        compiler_params=pltpu.CompilerParams(
            dimension_semantics=("arbitrary", "arbitrary"), vmem_limit_bytes=VMEM_LIMIT),
        name="mla_proj",
    )(x, wa, qn, kvn, wqT, wkn, wvT, wgkT, tq, tk)


def _causal_mask():
    r = lax.broadcasted_iota(jnp.int32, (TM, TM), 0)
    c = lax.broadcasted_iota(jnp.int32, (TM, TM), 1)
    return r <= c


def _attend_pairs(n_q, score_block, bias_row, value_block, gT_ref, og_ref):
    mask = _causal_mask()
    per_pair = []
    for pl_ in range(PAIRS_PER_STEP):
        per_pair.append([(pl_, qi, j, t == 0, t == qi)
                         for qi in range(n_q) for t, j in enumerate([qi] + list(range(qi)))])
    steps = [st for group in zip(*per_pair) for st in group]

    def step_scores(step):
        pl_, qi, j, first = step[:4]
        out = []
        for hh in range(2):
            s = score_block(pl_, qi, hh, j)
            if first:
                s = jnp.where(mask, s, NEG)
            out.append((s, jnp.max(s, axis=0, keepdims=True)))
        return out

    ahead = [step_scores(st) for st in steps[:LOOKAHEAD]]
    m = {}
    acc = {}
    for n, (pl_, qi, j, first, last) in enumerate(steps):
        s_cur = ahead.pop(0)
        if n + LOOKAHEAD < len(steps):
            ahead.append(step_scores(steps[n + LOOKAHEAD]))
        for hh in range(2):
            s, cm = s_cur[hh]
            key = (pl_, hh)
            if first:
                m[key] = cm
                p = jnp.exp2(s - cm).astype(BF16)
                acc[key] = _dot(value_block(pl_, qi, hh, j), p)
            else:
                b = bias_row(pl_, qi, hh, j)
                m_new = jnp.maximum(m[key], cm if b is None else cm + b)
                p = jnp.exp2(s - (m_new if b is None else m_new - b)).astype(BF16)
                acc[key] = acc[key] * jnp.exp2(m[key] - m_new) + _dot(value_block(pl_, qi, hh, j), p)
                m[key] = m_new
        if last:
            hd = acc[(pl_, 0)].shape[0] - BF16_ROWS
            outs = [acc[(pl_, hh)][:hd] / acc[(pl_, hh)][hd:hd + 1] for hh in range(2)]
            cols = slice(qi * TM, (qi + 1) * TM)
            feats = slice(pl_ * LANES, (pl_ + 1) * LANES)
            oT = jnp.concatenate(outs, axis=0) * gT_ref[0, feats, cols].astype(F32)
            og_ref[0, cols, feats] = oT.T.astype(BF16)


def _mla_attn_kernel(qT_ref, kn_ref, kr_ref, vT_ref, gT_ref, og_ref, kslab_ref):
    S = kn_ref.shape[1]
    lane = lax.broadcasted_iota(jnp.int32, (S, SLAB), 1)
    kr = kr_ref[0]
    for pl_ in range(PAIRS_PER_STEP):
        kn = kn_ref[0, :, pl_ * LANES:(pl_ + 1) * LANES]
        kslab_ref[2 * pl_] = jnp.where(lane < NOPE, kn, kr)
        kslab_ref[2 * pl_ + 1] = jnp.where(lane >= NOPE, kn, kr)

    ones = jnp.ones((BF16_ROWS, TM), BF16)
    zeros = jnp.zeros((SLAB - QK_ROWS, TM), BF16)

    def score_block(pl_, qi, hh, j):
        h = 2 * pl_ + hh
        q = qT_ref[0, h * QK_ROWS:(h + 1) * QK_ROWS, qi * TM:(qi + 1) * TM]
        q_slab = jnp.concatenate([q, zeros] if hh == 0 else [q[:ROPE], zeros, q[ROPE:]], axis=0)
        return _dot(kslab_ref[h, j * TM:(j + 1) * TM, :], q_slab)

    def bias_row(pl_, qi, hh, j):
        return None

    def value_block(pl_, qi, hh, j):
        h = 2 * pl_ + hh
        return jnp.concatenate([vT_ref[0, h * V_HD:(h + 1) * V_HD, j * TM:(j + 1) * TM], ones], axis=0)

    _attend_pairs(S // TM, score_block, bias_row, value_block, gT_ref, og_ref)


def _mla_attn(qT, kn, kr, vT, gT):
    B, _, S = qT.shape
    hs = 2 * PAIRS_PER_STEP
    return pl.pallas_call(
        _mla_attn_kernel,
        grid=(B, H // hs),
        in_specs=[
            pl.BlockSpec((1, hs * QK_ROWS, S), lambda b, p: (b, p, 0)),
            pl.BlockSpec((1, S, hs * NOPE), lambda b, p: (b, 0, p)),
            pl.BlockSpec((1, S, LANES), lambda b, p: (b, 0, 0)),
            pl.BlockSpec((1, hs * V_HD, S), lambda b, p: (b, p, 0)),
            pl.BlockSpec((1, hs * V_HD, S), lambda b, p: (b, p, 0)),
        ],
        out_specs=pl.BlockSpec((1, S, hs * V_HD), lambda b, p: (b, 0, p)),
        out_shape=jax.ShapeDtypeStruct((B, S, H * V_HD), BF16),
        scratch_shapes=[pltpu.VMEM((hs, S, SLAB), BF16)],
        compiler_params=pltpu.CompilerParams(
            dimension_semantics=("arbitrary", "arbitrary"), vmem_limit_bytes=VMEM_LIMIT),
        name="mla_attn",
    )(qT, kn, kr, vT, gT)


def _residual_layer_norm(og, x, wo, g, b):
    z = ALPHA * x + _dot(og, wo)
    mu = jnp.mean(z, axis=-1, keepdims=True)
    zc = z - mu
    var = jnp.mean(zc * zc, axis=-1, keepdims=True)
    return zc * lax.rsqrt(var + LN_EPS) * g + b


def _out_ln_kernel(og_hbm, x_hbm, wo_ref, g_ref, b_ref, o_ref, og_buf, x_buf, sem):
    i = pl.program_id(0)
    n = pl.num_programs(0)

    def tile_copies(tile, slot):
        rows = pl.ds(pl.multiple_of(tile * TM_OUT, TM_OUT), TM_OUT)
        return (pltpu.make_async_copy(og_hbm.at[rows], og_buf.at[slot], sem.at[0, slot]),
                pltpu.make_async_copy(x_hbm.at[rows], x_buf.at[slot], sem.at[1, slot]))

    @pl.when(i == 0)
    def _():
        for tile in range(RING_SLOTS - 1):
            for c in tile_copies(tile, tile):
                c.start()

    @pl.when(i + RING_SLOTS - 1 < n)
    def _():
        ahead = i + RING_SLOTS - 1
        for c in tile_copies(ahead, lax.rem(ahead, RING_SLOTS)):
            c.start()

    slot = lax.rem(i, RING_SLOTS)
    for c in tile_copies(i, slot):
        c.wait()
    for r in range(TM_OUT // LN_ROWS):
        rows = slice(r * LN_ROWS, (r + 1) * LN_ROWS)
        o_ref[rows, :] = _residual_layer_norm(og_buf[slot, rows, :], x_buf[slot, rows, :], wo_ref[...],
                                              g_ref[...], b_ref[...])


def _out_ln(og, x, wo, g, b):
    R, D = x.shape
    W = og.shape[1]
    assert R // TM_OUT >= RING_SLOTS - 1
    return pl.pallas_call(
        _out_ln_kernel,
        grid=(R // TM_OUT,),
        in_specs=[
            pl.BlockSpec(memory_space=pl.ANY),
            pl.BlockSpec(memory_space=pl.ANY),
            pl.BlockSpec((W, D), lambda i: (0, 0)),
            pl.BlockSpec((1, D), lambda i: (0, 0)),
            pl.BlockSpec((1, D), lambda i: (0, 0)),
        ],
        out_specs=pl.BlockSpec((TM_OUT, D), lambda i: (i, 0)),
        out_shape=jax.ShapeDtypeStruct((R, D), F32),
        scratch_shapes=[pltpu.VMEM((RING_SLOTS, TM_OUT, W), og.dtype),
                        pltpu.VMEM((RING_SLOTS, TM_OUT, D), x.dtype),
                        pltpu.SemaphoreType.DMA((2, RING_SLOTS))],
        compiler_params=pltpu.CompilerParams(
            dimension_semantics=("arbitrary",), vmem_limit_bytes=VMEM_LIMIT),
        name="out_ln",
    )(og, x, wo, g, b)


def _moba_proj_kernel(og_ref, xin_ref, wo_ref, lng_ref, lnb_ref, wk_ref, wvT_ref, wqT_ref, wgT_ref,
                      ck_ref, s1_ref, s2_ref, cT_ref, sT_ref,
                      x1_ref, k_ref, vT_ref, qT_ref, gT_ref, sb_ref, kmt_ref, xb_ref):
    t = pl.program_id(0)
    nb = kmt_ref.shape[0] // H
    i = lax.rem(jnp.maximum(t - 1, 0), nb)

    @pl.when(t == 0)
    def _():
        xb_ref[1] = jnp.zeros(xb_ref.shape[1:], BF16)

    @pl.when(i == 0)
    def _():
        kmt_ref[...] = jnp.zeros_like(kmt_ref)

    xb = xb_ref[lax.rem(t + 1, 2)]
    x1 = _residual_layer_norm(og_ref[0], xin_ref[0], wo_ref[...], lng_ref[...], lnb_ref[...])
    x1_ref[0] = x1
    qT = _dot_nt(wqT_ref[...], xb).reshape(H, HD_B, TM)
    k = _dot(xb, wk_ref[...])
    q_r = (qT * cT_ref[...][None] + _rot_rows(qT) * sT_ref[...][None]).reshape(H * HD_B, TM).astype(BF16)
    qT_ref[0] = q_r

    gT_ref[0] = _silu(_dot_nt(wgT_ref[...], xb)).astype(BF16)
    g = _dot(kmt_ref[...].astype(BF16), q_r).reshape(nb, H, TM)
    vT_ref[0] = _dot_nt(wvT_ref[...], xb).astype(BF16)

    parts = []
    for c in range(H * HD_B // LANES):
        kc = k[:, c * LANES:(c + 1) * LANES]
        parts.append(kc * ck_ref[...] + pltpu.roll(kc, LANES - HD_B // 2, 1) * s1_ref[...]
                     + pltpu.roll(kc, HD_B // 2, 1) * s2_ref[...])
    k_r = jnp.concatenate(parts, axis=1)
    k_ref[0] = k_r.astype(BF16)

    g_eff = [g[j] + jnp.where(j < i, 0.0, NEG) for j in range(nb)]
    rows = []
    for j in range(nb):
        rank = jnp.zeros((H, TM), F32)
        for j2 in range(nb):
            if j2 == j:
                continue
            beats = (g_eff[j2] >= g_eff[j]) if j2 < j else (g_eff[j2] > g_eff[j])
            rank = rank + jnp.where(beats, 1.0, 0.0)
        rows.append(jnp.where(rank < float(MOBA_TOPK), jnp.where(j < i, 0.0, NEG), NEG))
    sb_ref[0] = jnp.concatenate(rows, axis=0)

    k_mean = jnp.sum(k_r, axis=0, keepdims=True) * (1.0 / TM)
    head_of_lane = lax.broadcasted_iota(jnp.int32, (H, H * HD_B), 1) // HD_B
    head_of_row = lax.broadcasted_iota(jnp.int32, (H, H * HD_B), 0)
    kmt_ref[pl.ds(pl.multiple_of(i * H, H), H), :] = jnp.where(
        head_of_lane == head_of_row, jnp.broadcast_to(k_mean, (H, H * HD_B)), 0.0)

    xb_ref[lax.rem(t, 2)] = x1.astype(BF16)


def _moba_proj(og, x, wo, lng, lnb, wk, wvT, wqT, wgT, ck, s1, s2, cT, sT):
    B, S, D = x.shape
    n = S // TM
    W = H * HD_B
    tiles = B * n
    full = lambda a: pl.BlockSpec(a.shape, lambda t: (0,) * a.ndim)
    ln_tile = lambda t: jnp.minimum(t, tiles - 1)
    pr_tile = lambda t: jnp.maximum(t - 1, 0)
    ln_rows = lambda t: (ln_tile(t) // n, ln_tile(t) % n, 0)
    pr_rows = lambda t: (pr_tile(t) // n, pr_tile(t) % n, 0)
    pr_cols = lambda t: (pr_tile(t) // n, 0, pr_tile(t) % n)
    return pl.pallas_call(
        _moba_proj_kernel,
        grid=(tiles + 1,),
        in_specs=[
            pl.BlockSpec((1, TM, og.shape[2]), ln_rows),
            pl.BlockSpec((1, TM, D), ln_rows),
            full(wo), full(lng), full(lnb),
            full(wk), full(wvT), full(wqT), full(wgT),
            pl.BlockSpec((TM, LANES), lambda t: (pr_tile(t) % n, 0)),
            pl.BlockSpec((TM, LANES), lambda t: (pr_tile(t) % n, 0)),
            pl.BlockSpec((TM, LANES), lambda t: (pr_tile(t) % n, 0)),
            pl.BlockSpec((HD_B, TM), lambda t: (0, pr_tile(t) % n)),
            pl.BlockSpec((HD_B, TM), lambda t: (0, pr_tile(t) % n)),
        ],
        out_specs=[
            pl.BlockSpec((1, TM, D), ln_rows),
            pl.BlockSpec((1, TM, W), pr_rows),
            pl.BlockSpec((1, W, TM), pr_cols),
            pl.BlockSpec((1, W, TM), pr_cols),
            pl.BlockSpec((1, W, TM), pr_cols),
            pl.BlockSpec((1, n * H, TM), pr_cols),
        ],
        out_shape=[
            jax.ShapeDtypeStruct((B, S, D), F32),
            jax.ShapeDtypeStruct((B, S, W), BF16),
            jax.ShapeDtypeStruct((B, W, S), BF16),
            jax.ShapeDtypeStruct((B, W, S), BF16),
            jax.ShapeDtypeStruct((B, W, S), BF16),
            jax.ShapeDtypeStruct((B, n * H, S), F32),
        ],
        scratch_shapes=[pltpu.VMEM((n * H, W), F32), pltpu.VMEM((2, TM, D), BF16)],
        compiler_params=pltpu.CompilerParams(
            dimension_semantics=("arbitrary",), vmem_limit_bytes=VMEM_LIMIT),
        name="moba_proj",
    )(og, x, wo, lng, lnb, wk, wvT, wqT, wgT, ck, s1, s2, cT, sT)


def _moba_attn_kernel(qT_ref, k_ref, vT_ref, gT_ref, sb_ref, og_ref):
    S = k_ref.shape[1]
    first_head = 2 * PAIRS_PER_STEP * pl.program_id(1)
    zeros = jnp.zeros((HD_B, TM), BF16)

    ones = jnp.ones((BF16_ROWS, TM), BF16)

    def score_block(pl_, qi, hh, j):
        h = 2 * pl_ + hh
        q = qT_ref[0, h * HD_B:(h + 1) * HD_B, qi * TM:(qi + 1) * TM]
        q_slab = jnp.concatenate([q, zeros] if hh == 0 else [zeros, q], axis=0)
        return _dot(k_ref[0, j * TM:(j + 1) * TM, pl_ * LANES:(pl_ + 1) * LANES], q_slab)

    def bias_row(pl_, qi, hh, j):
        head = first_head + 2 * pl_ + hh
        return sb_ref[0, pl.ds(j * H + head, 1), qi * TM:(qi + 1) * TM]

    def value_block(pl_, qi, hh, j):
        h = 2 * pl_ + hh
        return jnp.concatenate([vT_ref[0, h * HD_B:(h + 1) * HD_B, j * TM:(j + 1) * TM], ones], axis=0)

    _attend_pairs(S // TM, score_block, bias_row, value_block, gT_ref, og_ref)


def _moba_attn(qT, k, vT, gT, sb):
    B, W, S = qT.shape
    hs = 2 * PAIRS_PER_STEP
    return pl.pallas_call(
        _moba_attn_kernel,
        grid=(B, H // hs),
        in_specs=[
            pl.BlockSpec((1, hs * HD_B, S), lambda b, p: (b, p, 0)),
            pl.BlockSpec((1, S, hs * HD_B), lambda b, p: (b, 0, p)),
            pl.BlockSpec((1, hs * HD_B, S), lambda b, p: (b, p, 0)),
            pl.BlockSpec((1, hs * HD_B, S), lambda b, p: (b, p, 0)),
            pl.BlockSpec((1, sb.shape[1], S), lambda b, p: (b, 0, 0)),
        ],
        out_specs=pl.BlockSpec((1, S, hs * HD_B), lambda b, p: (b, 0, p)),
        out_shape=jax.ShapeDtypeStruct((B, S, W), BF16),
        compiler_params=pltpu.CompilerParams(
            dimension_semantics=("arbitrary", "arbitrary"), vmem_limit_bytes=VMEM_LIMIT),
        name="moba_attn",
    )(qT, k, vT, gT, sb)


def _rope_tables(seq, dim):
    inv = THETA ** (-jnp.arange(0, dim, 2, dtype=F32) / dim)
    ang = jnp.arange(seq, dtype=F32)[:, None] * inv[None, :]
    ang = jnp.concatenate([ang, ang], axis=-1)
    return jnp.cos(ang), jnp.sin(ang)


def _mla_mixer(x, w_in, q_norm, kv_norm, w_uq, w_ukv, cos, sin):
    wa =w_in[:, :Q_LORA + KV_LORA].astype(BF16)
    w_kr = w_in[:, Q_LORA + KV_LORA:Q_LORA + KV_LORA + ROPE]
    w_g = w_in[:, Q_LORA + KV_LORA + ROPE:]
    wgkT = jnp.concatenate([w_g, w_kr], axis=1).astype(BF16).T
    wqT = w_uq.astype(BF16).T

    wkv = w_ukv.reshape(KV_LORA, H, NOPE + V_HD)
    wkn = wkv[..., :NOPE].reshape(KV_LORA, H * NOPE).astype(BF16)
    wvT = wkv[..., NOPE:].reshape(KV_LORA, H * V_HD).astype(BF16).T

    tk = jnp.concatenate([cos.T, sin.T], axis=0)
    tq = tk * MLA_SCALE

    qT, kn, kr, vT, gT = _mla_proj(x, wa, q_norm[None], kv_norm[None], wqT, wkn, wvT, wgkT, tq, tk)
    return _mla_attn(qT, kn, kr, vT, gT)


def _moba_mixer(og_prev, x_prev, w_o_prev, ln_g_prev, ln_b_prev, w_kv, w_in, cos, sin):
    S = x_prev.shape[1]
    W = H * HD_B
    half = HD_B // 2
    wk = w_kv[:, :W].astype(BF16)
    wvT = w_kv[:, W:].astype(BF16).T
    wqT = w_in[:, :W].astype(BF16).T
    wgT = w_in[:, W:].astype(BF16).T
    zeros = jnp.zeros((S, half), F32)
    ck = jnp.concatenate([cos, cos], axis=1)
    s1 = jnp.tile(jnp.concatenate([-sin[:, :half], zeros], axis=1), (1, 2))
    s2 = jnp.tile(jnp.concatenate([zeros, sin[:, half:]], axis=1), (1, 2))
    cT, sT = cos.T * MOBA_SCALE, sin.T * MOBA_SCALE
    x1, k, vT, qT, gT, sb = _moba_proj(og_prev, x_prev, w_o_prev.astype(BF16), ln_g_prev[None], ln_b_prev[None],
                                       wk, wvT, wqT, wgT, ck, s1, s2, cT, sT)
    return x1, _moba_attn(qT, k, vT, gT, sb)


def kernel(x, mla_w_in, mla_q_norm, mla_kv_norm, mla_w_uq, mla_w_ukv, mla_w_o,
           moba_w_kv, moba_w_in, moba_w_o, ln_g, ln_b):
    B, S, D = x.shape
    assert mla_w_in.shape[0] == 1 and moba_w_in.shape[0] == 1 and ln_g.shape[0] == DEPTH
    assert D == D_MODEL and S % TM_A == 0 and S % TM == 0 and (B * S) % TM_OUT == 0
    assert H % (2 * PAIRS_PER_STEP) == 0
    cos_a, sin_a = _rope_tables(S, ROPE)
    cos_b, sin_b = _rope_tables(S, HD_B)
    og_a = _mla_mixer(x, mla_w_in[0], mla_q_norm[0], mla_kv_norm[0], mla_w_uq[0], mla_w_ukv[0], cos_a, sin_a)
    x1, og_b = _moba_mixer(og_a, x, mla_w_o[0], ln_g[0], ln_b[0], moba_w_kv, moba_w_in[0], cos_b, sin_b)
    out = _out_ln(og_b.reshape(B * S, H * HD_B), x1.reshape(B * S, D), moba_w_o[0].astype(BF16),
                  ln_g[1][None], ln_b[1][None])
    return out.reshape(B, S, D)
```

```python
import jax
import jax.numpy as jnp
from jax import lax
from jax.experimental import pallas as pl
from jax.experimental.pallas import tpu as pltpu

D_MODEL = 1024
H = 16
NOPE = 64
ROPE = 32
V_HD = 64
Q_LORA = 768
KV_LORA = 256
HD_B = 64
MOBA_BLOCK = 256
MOBA_TOPK = 3
THETA = 10000.0
LN_EPS = 1e-5
RMS_EPS = 1e-6
DEPTH = 2
ALPHA = (2 * DEPTH) ** 0.25

TM = MOBA_BLOCK
TM_A = 1024
TM_OUT = 1024
LN_ROWS = 256
RING_SLOTS = 3
PAIRS_PER_STEP = 2
LOOKAHEAD = 2
LANES = 128
BF16_ROWS = 16
SLAB = LANES
QK_ROWS = NOPE + ROPE
LOG2E = 1.4426950408889634
MLA_SCALE = QK_ROWS ** -0.5 * LOG2E
MOBA_SCALE = HD_B ** -0.5 * LOG2E
NEG = -1e30
VMEM_LIMIT = 48 * 1024 * 1024

F32 = jnp.float32
BF16 = jnp.bfloat16
_NT = (((1,), (1,)), ((), ()))


def _dot(a, b):
    return jnp.dot(a, b, preferred_element_type=F32)


def _dot_nt(a, b):
    return lax.dot_general(a, b, _NT, preferred_element_type=F32)


def _rms_norm(c, gain):
    return c * lax.rsqrt(jnp.mean(c * c, axis=-1, keepdims=True) + RMS_EPS) * gain


def _silu(g):
    return g * jax.nn.sigmoid(g)


def _rot_rows(t):
    half = t.shape[-2] // 2
    return jnp.concatenate([-t[..., half:, :], t[..., :half, :]], axis=-2)


def _mla_proj_kernel(x_ref, wa_ref, qn_ref, kvn_ref, wqT_ref, wkn_ref, wvT_ref, wgkT_ref, tq_ref, tk_ref,
                     qT_ref, kn_ref, kr_ref, vT_ref, gT_ref):
    tm = x_ref.shape[1]
    xb = x_ref[0].astype(BF16)
    c_kv = _dot(xb, wa_ref[:, Q_LORA:])
    c_q = _dot(xb, wa_ref[:, :Q_LORA])
    gk = _dot_nt(wgkT_ref[...], xb)
    gT_ref[0] = _silu(gk[:H * V_HD]).astype(BF16)
    ckv_n = _rms_norm(c_kv, kvn_ref[...]).astype(BF16)
    cq_n = _rms_norm(c_q, qn_ref[...]).astype(BF16)
    kn_ref[0] = _dot(ckv_n, wkn_ref[...]).astype(BF16)
    vT_ref[0] = _dot_nt(wvT_ref[...], ckv_n).astype(BF16)
    qT = _dot_nt(wqT_ref[...], cq_n).reshape(H // 2, 2, QK_ROWS, tm)

    kr = gk[H * V_HD:]
    kr = kr * tk_ref[:ROPE, :] + _rot_rows(kr) * tk_ref[ROPE:, :]
    kr_ref[0] = jnp.concatenate([kr] * (LANES // ROPE), axis=0).T.astype(BF16)

    nope = qT[:, :, :NOPE] * MLA_SCALE
    rope = qT[:, :, NOPE:]
    roped = rope * tq_ref[:ROPE, :] + _rot_rows(rope) * tq_ref[ROPE:, :]
    pair = jnp.concatenate([nope[:, 0], roped[:, 0], roped[:, 1], nope[:, 1]], axis=1)
    qT_ref[0] = pair.reshape(H * QK_ROWS, tm).astype(BF16)


def _mla_proj(x, wa, qn, kvn, wqT, wkn, wvT, wgkT, tq, tk):
    B, S, D = x.shape
    n = S // TM_A
    full = lambda a: pl.BlockSpec(a.shape, lambda b, i: (0,) * a.ndim)
    return pl.pallas_call(
        _mla_proj_kernel,
        grid=(B, n),
        in_specs=[
            pl.BlockSpec((1, TM_A, D), lambda b, i: (b, i, 0)),
            full(wa), full(qn), full(kvn), full(wqT), full(wkn), full(wvT), full(wgkT),
            pl.BlockSpec((tq.shape[0], TM_A), lambda b, i: (0, i)),
            pl.BlockSpec((tk.shape[0], TM_A), lambda b, i: (0, i)),
        ],
        out_specs=[
            pl.BlockSpec((1, H * QK_ROWS, TM_A), lambda b, i: (b, 0, i)),
            pl.BlockSpec((1, TM_A, H * NOPE), lambda b, i: (b, i, 0)),
            pl.BlockSpec((1, TM_A, LANES), lambda b, i: (b, i, 0)),
            pl.BlockSpec((1, H * V_HD, TM_A), lambda b, i: (b, 0, i)),
            pl.BlockSpec((1, H * V_HD, TM_A), lambda b, i: (b, 0, i)),
        ],
        out_shape=[
            jax.ShapeDtypeStruct((B, H * QK_ROWS, S), BF16),
            jax.ShapeDtypeStruct((B, S, H * NOPE), BF16),
            jax.ShapeDtypeStruct((B, S, LANES), BF16),
            jax.ShapeDtypeStruct((B, H * V_HD, S), BF16),
            jax.ShapeDtypeStruct((B, H * V_HD, S), BF16),
        ],
        compiler_params=pltpu.CompilerParams(
            dimension_semantics=("arbitrary", "arbitrary"), vmem_limit_bytes=VMEM_LIMIT),
        name="mla_proj",
    )(x, wa, qn, kvn, wqT, wkn, wvT, wgkT, tq, tk)


def _causal_mask():
    r = lax.broadcasted_iota(jnp.int32, (TM, TM), 0)
    c = lax.broadcasted_iota(jnp.int32, (TM, TM), 1)
    return r <= c


def _attend_pairs(n_q, score_block, bias_row, value_block, gT_ref, og_ref):
    mask = _causal_mask()
    per_pair = []
    for pl_ in range(PAIRS_PER_STEP):
        per_pair.append([(pl_, qi, j, t == 0, t == qi)
                         for qi in range(n_q) for t, j in enumerate([qi] + list(range(qi)))])
    steps = [st for group in zip(*per_pair) for st in group]

    def step_scores(step):
        pl_, qi, j, first = step[:4]
        out = []
        for hh in range(2):
            s = score_block(pl_, qi, hh, j)
            if first:
                s = jnp.where(mask, s, NEG)
            out.append((s, jnp.max(s, axis=0, keepdims=True)))
        return out

    ahead = [step_scores(st) for st in steps[:LOOKAHEAD]]
    m = {}
    acc = {}
    for n, (pl_, qi, j, first, last) in enumerate(steps):
        s_cur = ahead.pop(0)
        if n + LOOKAHEAD < len(steps):
            ahead.append(step_scores(steps[n + LOOKAHEAD]))
        for hh in range(2):
            s, cm = s_cur[hh]
            key = (pl_, hh)
            if first:
                m[key] = cm
                p = jnp.exp2(s - cm).astype(BF16)
                acc[key] = _dot(value_block(pl_, qi, hh, j), p)
            else:
                b = bias_row(pl_, qi, hh, j)
                m_new = jnp.maximum(m[key], cm if b is None else cm + b)
                p = jnp.exp2(s - (m_new if b is None else m_new - b)).astype(BF16)
                acc[key] = acc[key] * jnp.exp2(m[key] - m_new) + _dot(value_block(pl_, qi, hh, j), p)
                m[key] = m_new
        if last:
            hd = acc[(pl_, 0)].shape[0] - BF16_ROWS
            outs = [acc[(pl_, hh)][:hd] / acc[(pl_, hh)][hd:hd + 1] for hh in range(2)]
            cols = slice(qi * TM, (qi + 1) * TM)
            feats = slice(pl_ * LANES, (pl_ + 1) * LANES)
            oT = jnp.concatenate(outs, axis=0) * gT_ref[0, feats, cols].astype(F32)
            og_ref[0, cols, feats] = oT.T.astype(BF16)


def _mla_attn_kernel(qT_ref, kn_ref, kr_ref, vT_ref, gT_ref, og_ref, kslab_ref):
    S = kn_ref.shape[1]
    lane = lax.broadcasted_iota(jnp.int32, (S, SLAB), 1)
    kr = kr_ref[0]
    for pl_ in range(PAIRS_PER_STEP):
        kn = kn_ref[0, :, pl_ * LANES:(pl_ + 1) * LANES]
        kslab_ref[2 * pl_] = jnp.where(lane < NOPE, kn, kr)
        kslab_ref[2 * pl_ + 1] = jnp.where(lane >= NOPE, kn, kr)

    ones = jnp.ones((BF16_ROWS, TM), BF16)
    zeros = jnp.zeros((SLAB - QK_ROWS, TM), BF16)

    def score_block(pl_, qi, hh, j):
        h = 2 * pl_ + hh
        q = qT_ref[0, h * QK_ROWS:(h + 1) * QK_ROWS, qi * TM:(qi + 1) * TM]
        q_slab = jnp.concatenate([q, zeros] if hh == 0 else [q[:ROPE], zeros, q[ROPE:]], axis=0)
        return _dot(kslab_ref[h, j * TM:(j + 1) * TM, :], q_slab)

    def bias_row(pl_, qi, hh, j):
        return None

    def value_block(pl_, qi, hh, j):
        h = 2 * pl_ + hh
        return jnp.concatenate([vT_ref[0, h * V_HD:(h + 1) * V_HD, j * TM:(j + 1) * TM], ones], axis=0)

    _attend_pairs(S // TM, score_block, bias_row, value_block, gT_ref, og_ref)


def _mla_attn(qT, kn, kr, vT, gT):
    B, _, S = qT.shape
    hs = 2 * PAIRS_PER_STEP
    return pl.pallas_call(
        _mla_attn_kernel,
        grid=(B, H // hs),
        in_specs=[
            pl.BlockSpec((1, hs * QK_ROWS, S), lambda b, p: (b, p, 0)),
            pl.BlockSpec((1, S, hs * NOPE), lambda b, p: (b, 0, p)),
            pl.BlockSpec((1, S, LANES), lambda b, p: (b, 0, 0)),
            pl.BlockSpec((1, hs * V_HD, S), lambda b, p: (b, p, 0)),
            pl.BlockSpec((1, hs * V_HD, S), lambda b, p: (b, p, 0)),
        ],
        out_specs=pl.BlockSpec((1, S, hs * V_HD), lambda b, p: (b, 0, p)),
        out_shape=jax.ShapeDtypeStruct((B, S, H * V_HD), BF16),
        scratch_shapes=[pltpu.VMEM((hs, S, SLAB), BF16)],
        compiler_params=pltpu.CompilerParams(
            dimension_semantics=("arbitrary", "arbitrary"), vmem_limit_bytes=VMEM_LIMIT),
        name="mla_attn",
    )(qT, kn, kr, vT, gT)


def _residual_layer_norm(og, x, wo, g, b):
    z = ALPHA * x + _dot(og, wo)
    mu = jnp.mean(z, axis=-1, keepdims=True)
    zc = z - mu
    var = jnp.mean(zc * zc, axis=-1, keepdims=True)
    return zc * lax.rsqrt(var + LN_EPS) * g + b


def _out_ln_kernel(og_hbm, x_hbm, wo_ref, g_ref, b_ref, o_ref, og_buf, x_buf, sem):
    i = pl.program_id(0)
    n = pl.num_programs(0)

    def tile_copies(tile, slot):
        rows = pl.ds(pl.multiple_of(tile * TM_OUT, TM_OUT), TM_OUT)
        return (pltpu.make_async_copy(og_hbm.at[rows], og_buf.at[slot], sem.at[0, slot]),
                pltpu.make_async_copy(x_hbm.at[rows], x_buf.at[slot], sem.at[1, slot]))

    @pl.when(i == 0)
    def _():
        for tile in range(RING_SLOTS - 1):
            for c in tile_copies(tile, tile):
                c.start()

    @pl.when(i + RING_SLOTS - 1 < n)
    def _():
        ahead = i + RING_SLOTS - 1
        for c in tile_copies(ahead, lax.rem(ahead, RING_SLOTS)):
            c.start()

    slot = lax.rem(i, RING_SLOTS)
    for c in tile_copies(i, slot):
        c.wait()
    for r in range(TM_OUT // LN_ROWS):
        rows = slice(r * LN_ROWS, (r + 1) * LN_ROWS)
        o_ref[rows, :] = _residual_layer_norm(og_buf[slot, rows, :], x_buf[slot, rows, :], wo_ref[...],
                                              g_ref[...], b_ref[...])


def _out_ln(og, x, wo, g, b):
    R, D = x.shape
    W = og.shape[1]
    assert R // TM_OUT >= RING_SLOTS - 1
    return pl.pallas_call(
        _out_ln_kernel,
        grid=(R // TM_OUT,),
        in_specs=[
            pl.BlockSpec(memory_space=pl.ANY),
            pl.BlockSpec(memory_space=pl.ANY),
            pl.BlockSpec((W, D), lambda i: (0, 0)),
            pl.BlockSpec((1, D), lambda i: (0, 0)),
            pl.BlockSpec((1, D), lambda i: (0, 0)),
        ],
        out_specs=pl.BlockSpec((TM_OUT, D), lambda i: (i, 0)),
        out_shape=jax.ShapeDtypeStruct((R, D), F32),
        scratch_shapes=[pltpu.VMEM((RING_SLOTS, TM_OUT, W), og.dtype),
                        pltpu.VMEM((RING_SLOTS, TM_OUT, D), x.dtype),
                        pltpu.SemaphoreType.DMA((2, RING_SLOTS))],
        compiler_params=pltpu.CompilerParams(
            dimension_semantics=("arbitrary",), vmem_limit_bytes=VMEM_LIMIT),
        name="out_ln",
    )(og, x, wo, g, b)


def _moba_proj_kernel(og_ref, xin_ref, wo_ref, lng_ref, lnb_ref, wk_ref, wvT_ref, wqT_ref, wgT_ref,
                      ck_ref, s1_ref, s2_ref, cT_ref, sT_ref,
                      x1_ref, k_ref, vT_ref, qT_ref, gT_ref, sb_ref, kmt_ref, xb_ref):
    t = pl.program_id(0)
    nb = kmt_ref.shape[0] // H
    i = lax.rem(jnp.maximum(t - 1, 0), nb)

    @pl.when(t == 0)
    def _():
        xb_ref[1] = jnp.zeros(xb_ref.shape[1:], BF16)

    @pl.when(i == 0)
    def _():
        kmt_ref[...] = jnp.zeros_like(kmt_ref)

    xb = xb_ref[lax.rem(t + 1, 2)]
    x1 = _residual_layer_norm(og_ref[0], xin_ref[0], wo_ref[...], lng_ref[...], lnb_ref[...])
    x1_ref[0] = x1
    qT = _dot_nt(wqT_ref[...], xb).reshape(H, HD_B, TM)
    k = _dot(xb, wk_ref[...])
    q_r = (qT * cT_ref[...][None] + _rot_rows(qT) * sT_ref[...][None]).reshape(H * HD_B, TM).astype(BF16)
    qT_ref[0] = q_r

    gT_ref[0] = _silu(_dot_nt(wgT_ref[...], xb)).astype(BF16)
    g = _dot(kmt_ref[...].astype(BF16), q_r).reshape(nb, H, TM)
    vT_ref[0] = _dot_nt(wvT_ref[...], xb).astype(BF16)

    parts = []
    for c in range(H * HD_B // LANES):
        kc = k[:, c * LANES:(c + 1) * LANES]
        parts.append(kc * ck_ref[...] + pltpu.roll(kc, LANES - HD_B // 2, 1) * s1_ref[...]
                     + pltpu.roll(kc, HD_B // 2, 1) * s2_ref[...])
    k_r = jnp.concatenate(parts, axis=1)
    k_ref[0] = k_r.astype(BF16)

    g_eff = [g[j] + jnp.where(j < i, 0.0, NEG) for j in range(nb)]
    rows = []
    for j in range(nb):
        rank = jnp.zeros((H, TM), F32)
        for j2 in range(nb):
            if j2 == j:
                continue
            beats = (g_eff[j2] >= g_eff[j]) if j2 < j else (g_eff[j2] > g_eff[j])
            rank = rank + jnp.where(beats, 1.0, 0.0)
        rows.append(jnp.where(rank < float(MOBA_TOPK), jnp.where(j < i, 0.0, NEG), NEG))
    sb_ref[0] = jnp.concatenate(rows, axis=0)

    k_mean = jnp.sum(k_r, axis=0, keepdims=True) * (1.0 / TM)
    head_of_lane = lax.broadcasted_iota(jnp.int32, (H, H * HD_B), 1) // HD_B
    head_of_row = lax.broadcasted_iota(jnp.int32, (H, H * HD_B), 0)
    kmt_ref[pl.ds(pl.multiple_of(i * H, H), H), :] = jnp.where(
        head_of_lane == head_of_row, jnp.broadcast_to(k_mean, (H, H * HD_B)), 0.0)

    xb_ref[lax.rem(t, 2)] = x1.astype(BF16)


def _moba_proj(og, x, wo, lng, lnb, wk, wvT, wqT, wgT, ck, s1, s2, cT, sT):
    B, S, D = x.shape
    n = S // TM
    W = H * HD_B
    tiles = B * n
    full = lambda a: pl.BlockSpec(a.shape, lambda t: (0,) * a.ndim)
    ln_tile = lambda t: jnp.minimum(t, tiles - 1)
    pr_tile = lambda t: jnp.maximum(t - 1, 0)
    ln_rows = lambda t: (ln_tile(t) // n, ln_tile(t) % n, 0)
    pr_rows = lambda t: (pr_tile(t) // n, pr_tile(t) % n, 0)
    pr_cols = lambda t: (pr_tile(t) // n, 0, pr_tile(t) % n)
    return pl.pallas_call(
        _moba_proj_kernel,
        grid=(tiles + 1,),
        in_specs=[
            pl.BlockSpec((1, TM, og.shape[2]), ln_rows),
            pl.BlockSpec((1, TM, D), ln_rows),
            full(wo), full(lng), full(lnb),
            full(wk), full(wvT), full(wqT), full(wgT),
            pl.BlockSpec((TM, LANES), lambda t: (pr_tile(t) % n, 0)),
            pl.BlockSpec((TM, LANES), lambda t: (pr_tile(t) % n, 0)),
            pl.BlockSpec((TM, LANES), lambda t: (pr_tile(t) % n, 0)),
            pl.BlockSpec((HD_B, TM), lambda t: (0, pr_tile(t) % n)),
            pl.BlockSpec((HD_B, TM), lambda t: (0, pr_tile(t) % n)),
        ],
        out_specs=[
            pl.BlockSpec((1, TM, D), ln_rows),
            pl.BlockSpec((1, TM, W), pr_rows),
            pl.BlockSpec((1, W, TM), pr_cols),
            pl.BlockSpec((1, W, TM), pr_cols),
            pl.BlockSpec((1, W, TM), pr_cols),
            pl.BlockSpec((1, n * H, TM), pr_cols),
        ],
        out_shape=[
            jax.ShapeDtypeStruct((B, S, D), F32),
            jax.ShapeDtypeStruct((B, S, W), BF16),
            jax.ShapeDtypeStruct((B, W, S), BF16),
            jax.ShapeDtypeStruct((B, W, S), BF16),
            jax.ShapeDtypeStruct((B, W, S), BF16),
            jax.ShapeDtypeStruct((B, n * H, S), F32),
        ],
        scratch_shapes=[pltpu.VMEM((n * H, W), F32), pltpu.VMEM((2, TM, D), BF16)],
        compiler_params=pltpu.CompilerParams(
            dimension_semantics=("arbitrary",), vmem_limit_bytes=VMEM_LIMIT),
        name="moba_proj",
    )(og, x, wo, lng, lnb, wk, wvT, wqT, wgT, ck, s1, s2, cT, sT)


def _moba_attn_kernel(qT_ref, k_ref, vT_ref, gT_ref, sb_ref, og_ref):
    S = k_ref.shape[1]
    first_head = 2 * PAIRS_PER_STEP * pl.program_id(1)
    zeros = jnp.zeros((HD_B, TM), BF16)

    ones = jnp.ones((BF16_ROWS, TM), BF16)

    def score_block(pl_, qi, hh, j):
        h = 2 * pl_ + hh
        q = qT_ref[0, h * HD_B:(h + 1) * HD_B, qi * TM:(qi + 1) * TM]
        q_slab = jnp.concatenate([q, zeros] if hh == 0 else [zeros, q], axis=0)
        return _dot(k_ref[0, j * TM:(j + 1) * TM, pl_ * LANES:(pl_ + 1) * LANES], q_slab)

    def bias_row(pl_, qi, hh, j):
        head = first_head + 2 * pl_ + hh
        return sb_ref[0, pl.ds(j * H + head, 1), qi * TM:(qi + 1) * TM]

    def value_block(pl_, qi, hh, j):
        h = 2 * pl_ + hh
        return jnp.concatenate([vT_ref[0, h * HD_B:(h + 1) * HD_B, j * TM:(j + 1) * TM], ones], axis=0)

    _attend_pairs(S // TM, score_block, bias_row, value_block, gT_ref, og_ref)


def _moba_attn(qT, k, vT, gT, sb):
    B, W, S = qT.shape
    hs = 2 * PAIRS_PER_STEP
    return pl.pallas_call(
        _moba_attn_kernel,
        grid=(B, H // hs),
        in_specs=[
            pl.BlockSpec((1, hs * HD_B, S), lambda b, p: (b, p, 0)),
            pl.BlockSpec((1, S, hs * HD_B), lambda b, p: (b, 0, p)),
            pl.BlockSpec((1, hs * HD_B, S), lambda b, p: (b, p, 0)),
            pl.BlockSpec((1, hs * HD_B, S), lambda b, p: (b, p, 0)),
            pl.BlockSpec((1, sb.shape[1], S), lambda b, p: (b, 0, 0)),
        ],
        out_specs=pl.BlockSpec((1, S, hs * HD_B), lambda b, p: (b, 0, p)),
        out_shape=jax.ShapeDtypeStruct((B, S, W), BF16),
        compiler_params=pltpu.CompilerParams(
            dimension_semantics=("arbitrary", "arbitrary"), vmem_limit_bytes=VMEM_LIMIT),
        name="moba_attn",
    )(qT, k, vT, gT, sb)


def _rope_tables(seq, dim):
    inv = THETA ** (-jnp.arange(0, dim, 2, dtype=F32) / dim)
    ang = jnp.arange(seq, dtype=F32)[:, None] * inv[None, :]
    ang = jnp.concatenate([ang, ang], axis=-1)
    return jnp.cos(ang), jnp.sin(ang)


def _mla_mixer(x, w_in, q_norm, kv_norm, w_uq, w_ukv, cos, sin):
    wa =w_in[:, :Q_LORA + KV_LORA].astype(BF16)
    w_kr = w_in[:, Q_LORA + KV_LORA:Q_LORA + KV_LORA + ROPE]
    w_g = w_in[:, Q_LORA + KV_LORA + ROPE:]
    wgkT = jnp.concatenate([w_g, w_kr], axis=1).astype(BF16).T
    wqT = w_uq.astype(BF16).T

    wkv = w_ukv.reshape(KV_LORA, H, NOPE + V_HD)
    wkn = wkv[..., :NOPE].reshape(KV_LORA, H * NOPE).astype(BF16)
    wvT = wkv[..., NOPE:].reshape(KV_LORA, H * V_HD).astype(BF16).T

    tk = jnp.concatenate([cos.T, sin.T], axis=0)
    tq = tk * MLA_SCALE

    qT, kn, kr, vT, gT = _mla_proj(x, wa, q_norm[None], kv_norm[None], wqT, wkn, wvT, wgkT, tq, tk)
    return _mla_attn(qT, kn, kr, vT, gT)


def _moba_mixer(og_prev, x_prev, w_o_prev, ln_g_prev, ln_b_prev, w_kv, w_in, cos, sin):
    S = x_prev.shape[1]
    W = H * HD_B
    half = HD_B // 2
    wk = w_kv[:, :W].astype(BF16)
    wvT = w_kv[:, W:].astype(BF16).T
    wqT = w_in[:, :W].astype(BF16).T
    wgT = w_in[:, W:].astype(BF16).T
    zeros = jnp.zeros((S, half), F32)
    ck = jnp.concatenate([cos, cos], axis=1)
    s1 = jnp.tile(jnp.concatenate([-sin[:, :half], zeros], axis=1), (1, 2))
    s2 = jnp.tile(jnp.concatenate([zeros, sin[:, half:]], axis=1), (1, 2))
    cT, sT = cos.T * MOBA_SCALE, sin.T * MOBA_SCALE
    x1, k, vT, qT, gT, sb = _moba_proj(og_prev, x_prev, w_o_prev.astype(BF16), ln_g_prev[None], ln_b_prev[None],
                                       wk, wvT, wqT, wgT, ck, s1, s2, cT, sT)
    return x1, _moba_attn(qT, k, vT, gT, sb)


def kernel(x, mla_w_in, mla_q_norm, mla_kv_norm, mla_w_uq, mla_w_ukv, mla_w_o,
           moba_w_kv, moba_w_in, moba_w_o, ln_g, ln_b):
    B, S, D = x.shape
    assert mla_w_in.shape[0] == 1 and moba_w_in.shape[0] == 1 and ln_g.shape[0] == DEPTH
    assert D == D_MODEL and S % TM_A == 0 and S % TM == 0 and (B * S) % TM_OUT == 0
    assert H % (2 * PAIRS_PER_STEP) == 0
    cos_a, sin_a = _rope_tables(S, ROPE)
    cos_b, sin_b = _rope_tables(S, HD_B)
    og_a = _mla_mixer(x, mla_w_in[0], mla_q_norm[0], mla_kv_norm[0], mla_w_uq[0], mla_w_ukv[0], cos_a, sin_a)
    x1, og_b = _moba_mixer(og_a, x, mla_w_o[0], ln_g[0], ln_b[0], moba_w_kv, moba_w_in[0], cos_b, sin_b)
    out = _out_ln(og_b.reshape(B * S, H * HD_B), x1.reshape(B * S, D), moba_w_o[0].astype(BF16),
                  ln_g[1][None], ln_b[1][None])
    return out.reshape(B, S, D)
```

```python
import jax
import jax.numpy as jnp
from jax import lax
from jax.experimental import pallas as pl
from jax.experimental.pallas import tpu as pltpu

D_MODEL = 1024
H = 16
NOPE = 64
ROPE = 32
V_HD = 64
Q_LORA = 768
KV_LORA = 256
HD_B = 64
MOBA_BLOCK = 256
MOBA_TOPK = 3
THETA = 10000.0
LN_EPS = 1e-5
RMS_EPS = 1e-6
DEPTH = 2
ALPHA = (2 * DEPTH) ** 0.25

TM = MOBA_BLOCK
TM_A = 512
TM_OUT = 1024
LN_ROWS = 256
RING_SLOTS = 3
PAIRS_PER_STEP = 2
LOOKAHEAD = 2
LANES = 128
BF16_ROWS = 16
SLAB = LANES
QK_ROWS = NOPE + ROPE
LOG2E = 1.4426950408889634
MLA_SCALE = QK_ROWS ** -0.5 * LOG2E
MOBA_SCALE = HD_B ** -0.5 * LOG2E
NEG = -1e30
VMEM_LIMIT = 48 * 1024 * 1024

F32 = jnp.float32
BF16 = jnp.bfloat16
_NT = (((1,), (1,)), ((), ()))


def _dot(a, b):
    return jnp.dot(a, b, preferred_element_type=F32)


def _dot_nt(a, b):
    return lax.dot_general(a, b, _NT, preferred_element_type=F32)


def _rms_norm(c, gain):
    return c * lax.rsqrt(jnp.mean(c * c, axis=-1, keepdims=True) + RMS_EPS) * gain


def _silu(g):
    return g * jax.nn.sigmoid(g)


def _rot_rows(t):
    half = t.shape[-2] // 2
    return jnp.concatenate([-t[..., half:, :], t[..., :half, :]], axis=-2)


def _mla_proj_kernel(x_ref, wa_ref, qn_ref, kvn_ref, wqT_ref, wkn_ref, wvT_ref, wgkT_ref, tq_ref, tk_ref,
                     qT_ref, ks_ref, vT_ref, gT_ref):
    tm = x_ref.shape[1]
    xb = x_ref[0].astype(BF16)
    c_kv = _dot(xb, wa_ref[:, Q_LORA:])
    c_q = _dot(xb, wa_ref[:, :Q_LORA])
    gk = _dot_nt(wgkT_ref[...], xb)
    gT_ref[0] = _silu(gk[:H * V_HD]).astype(BF16)
    ckv_n = _rms_norm(c_kv, kvn_ref[...]).astype(BF16)
    cq_n = _rms_norm(c_q, qn_ref[...]).astype(BF16)
    kn = _dot(ckv_n, wkn_ref[...])
    vT_ref[0] = _dot_nt(wvT_ref[...], ckv_n).astype(BF16)
    qT = _dot_nt(wqT_ref[...], cq_n).reshape(H // 2, 2, QK_ROWS, tm)

    kr = gk[H * V_HD:]
    kr = kr * tk_ref[:ROPE, :] + _rot_rows(kr) * tk_ref[ROPE:, :]
    kr = jnp.concatenate([kr] * (LANES // ROPE), axis=0).T
    lane = lax.broadcasted_iota(jnp.int32, (tm, LANES), 1)
    slabs = []
    for p in range(H // 2):
        kn_pair = kn[:, p * LANES:(p + 1) * LANES]
        slabs += [jnp.where(lane < NOPE, kn_pair, kr), jnp.where(lane >= NOPE, kn_pair, kr)]
    ks_ref[0] = jnp.concatenate(slabs, axis=1).astype(BF16)

    nope = qT[:, :, :NOPE] * MLA_SCALE
    rope = qT[:, :, NOPE:]
    roped = rope * tq_ref[:ROPE, :] + _rot_rows(rope) * tq_ref[ROPE:, :]
    pair = jnp.concatenate([nope[:, 0], roped[:, 0], roped[:, 1], nope[:, 1]], axis=1)
    qT_ref[0] = pair.reshape(H * QK_ROWS, tm).astype(BF16)


def _mla_proj(x, wa, qn, kvn, wqT, wkn, wvT, wgkT, tq, tk):
    B, S, D = x.shape
    n = S // TM_A
    full = lambda a: pl.BlockSpec(a.shape, lambda b, i: (0,) * a.ndim)
    return pl.pallas_call(
        _mla_proj_kernel,
        grid=(B, n),
        in_specs=[
            pl.BlockSpec((1, TM_A, D), lambda b, i: (b, i, 0)),
            full(wa), full(qn), full(kvn), full(wqT), full(wkn), full(wvT), full(wgkT),
            pl.BlockSpec((tq.shape[0], TM_A), lambda b, i: (0, i)),
            pl.BlockSpec((tk.shape[0], TM_A), lambda b, i: (0, i)),
        ],
        out_specs=[
            pl.BlockSpec((1, H * QK_ROWS, TM_A), lambda b, i: (b, 0, i)),
            pl.BlockSpec((1, TM_A, H * SLAB), lambda b, i: (b, i, 0)),
            pl.BlockSpec((1, H * V_HD, TM_A), lambda b, i: (b, 0, i)),
            pl.BlockSpec((1, H * V_HD, TM_A), lambda b, i: (b, 0, i)),
        ],
        out_shape=[
            jax.ShapeDtypeStruct((B, H * QK_ROWS, S), BF16),
            jax.ShapeDtypeStruct((B, S, H * SLAB), BF16),
            jax.ShapeDtypeStruct((B, H * V_HD, S), BF16),
            jax.ShapeDtypeStruct((B, H * V_HD, S), BF16),
        ],
        compiler_params=pltpu.CompilerParams(
            dimension_semantics=("arbitrary", "arbitrary"), vmem_limit_bytes=VMEM_LIMIT),
        name="mla_proj",
    )(x, wa, qn, kvn, wqT, wkn, wvT, wgkT, tq, tk)


def _causal_mask():
    r = lax.broadcasted_iota(jnp.int32, (TM, TM), 0)
    c = lax.broadcasted_iota(jnp.int32, (TM, TM), 1)
    return r <= c


def _attend_pairs(n_q, score_block, bias_row, value_block, gT_ref, og_ref):
    mask = _causal_mask()
    per_pair = []
    for pl_ in range(PAIRS_PER_STEP):
        per_pair.append([(pl_, qi, j, t == 0, t == qi)
                         for qi in range(n_q) for t, j in enumerate([qi] + list(range(qi)))])
    steps = [st for group in zip(*per_pair) for st in group]

    def step_scores(step):
        pl_, qi, j, first = step[:4]
        out = []
        for hh in range(2):
            s = score_block(pl_, qi, hh, j)
            if first:
                s = jnp.where(mask, s, NEG)
            out.append((s, jnp.max(s, axis=0, keepdims=True)))
        return out

    ahead = [step_scores(st) for st in steps[:LOOKAHEAD]]
    m = {}
    acc = {}
    for n, (pl_, qi, j, first, last) in enumerate(steps):
        s_cur = ahead.pop(0)
        if n + LOOKAHEAD < len(steps):
            ahead.append(step_scores(steps[n + LOOKAHEAD]))
        for hh in range(2):
            s, cm = s_cur[hh]
            key = (pl_, hh)
            if first:
                m[key] = cm
                p = jnp.exp2(s - cm).astype(BF16)
                acc[key] = _dot(value_block(pl_, qi, hh, j), p)
            else:
                b = bias_row(pl_, qi, hh, j)
                m_new = jnp.maximum(m[key], cm if b is None else cm + b)
                p = jnp.exp2(s - (m_new if b is None else m_new - b)).astype(BF16)
                acc[key] = acc[key] * jnp.exp2(m[key] - m_new) + _dot(value_block(pl_, qi, hh, j), p)
                m[key] = m_new
        if last:
            hd = acc[(pl_, 0)].shape[0] - BF16_ROWS
            outs = [acc[(pl_, hh)][:hd] / acc[(pl_, hh)][hd:hd + 1] for hh in range(2)]
            cols = slice(qi * TM, (qi + 1) * TM)
            feats = slice(pl_ * LANES, (pl_ + 1) * LANES)
            oT = jnp.concatenate(outs, axis=0) * gT_ref[0, feats, cols].astype(F32)
            og_ref[0, cols, feats] = oT.T.astype(BF16)


def _mla_attn_kernel(qT_ref, ks_ref, vT_ref, gT_ref, og_ref):
    S = ks_ref.shape[1]
    ones = jnp.ones((BF16_ROWS, TM), BF16)
    zeros = jnp.zeros((SLAB - QK_ROWS, TM), BF16)

    def score_block(pl_, qi, hh, j):
        h = 2 * pl_ + hh
        q = qT_ref[0, h * QK_ROWS:(h + 1) * QK_ROWS, qi * TM:(qi + 1) * TM]
        q_slab = jnp.concatenate([q, zeros] if hh == 0 else [q[:ROPE], zeros, q[ROPE:]], axis=0)
        return _dot(ks_ref[0, j * TM:(j + 1) * TM, h * SLAB:(h + 1) * SLAB], q_slab)

    def bias_row(pl_, qi, hh, j):
        return None

    def value_block(pl_, qi, hh, j):
        h = 2 * pl_ + hh
        return jnp.concatenate([vT_ref[0, h * V_HD:(h + 1) * V_HD, j * TM:(j + 1) * TM], ones], axis=0)

    _attend_pairs(S // TM, score_block, bias_row, value_block, gT_ref, og_ref)


def _mla_attn(qT, ks, vT, gT):
    B, _, S = qT.shape
    hs = 2 * PAIRS_PER_STEP
    return pl.pallas_call(
        _mla_attn_kernel,
        grid=(B, H // hs),
        in_specs=[
            pl.BlockSpec((1, hs * QK_ROWS, S), lambda b, p: (b, p, 0)),
            pl.BlockSpec((1, S, hs * SLAB), lambda b, p: (b, 0, p)),
            pl.BlockSpec((1, hs * V_HD, S), lambda b, p: (b, p, 0)),
            pl.BlockSpec((1, hs * V_HD, S), lambda b, p: (b, p, 0)),
        ],
        out_specs=pl.BlockSpec((1, S, hs * V_HD), lambda b, p: (b, 0, p)),
        out_shape=jax.ShapeDtypeStruct((B, S, H * V_HD), BF16),
        compiler_params=pltpu.CompilerParams(
            dimension_semantics=("arbitrary", "arbitrary"), vmem_limit_bytes=VMEM_LIMIT),
        name="mla_attn",
    )(qT, ks, vT, gT)


def _residual_layer_norm(og, x, wo, g, b):
    z = ALPHA * x + _dot(og, wo)
    mu = jnp.mean(z, axis=-1, keepdims=True)
    zc = z - mu
    var = jnp.mean(zc * zc, axis=-1, keepdims=True)
    return zc * lax.rsqrt(var + LN_EPS) * g + b


def _out_ln_kernel(og_hbm, x_hbm, wo_ref, g_ref, b_ref, o_ref, og_buf, x_buf, sem):
    i = pl.program_id(0)
    n = pl.num_programs(0)

    def tile_copies(tile, slot):
        rows = pl.ds(pl.multiple_of(tile * TM_OUT, TM_OUT), TM_OUT)
        return (pltpu.make_async_copy(og_hbm.at[rows], og_buf.at[slot], sem.at[0, slot]),
                pltpu.make_async_copy(x_hbm.at[rows], x_buf.at[slot], sem.at[1, slot]))

    @pl.when(i == 0)
    def _():
        for tile in range(RING_SLOTS - 1):
            for c in tile_copies(tile, tile):
                c.start()

    @pl.when(i + RING_SLOTS - 1 < n)
    def _():
        ahead = i + RING_SLOTS - 1
        for c in tile_copies(ahead, lax.rem(ahead, RING_SLOTS)):
            c.start()

    slot = lax.rem(i, RING_SLOTS)
    for c in tile_copies(i, slot):
        c.wait()
    for r in range(TM_OUT // LN_ROWS):
        rows = slice(r * LN_ROWS, (r + 1) * LN_ROWS)
        o_ref[rows, :] = _residual_layer_norm(og_buf[slot, rows, :], x_buf[slot, rows, :], wo_ref[...],
                                              g_ref[...], b_ref[...])


def _out_ln(og, x, wo, g, b):
    R, D = x.shape
    W = og.shape[1]
    assert R // TM_OUT >= RING_SLOTS - 1
    return pl.pallas_call(
        _out_ln_kernel,
        grid=(R // TM_OUT,),
        in_specs=[
            pl.BlockSpec(memory_space=pl.ANY),
            pl.BlockSpec(memory_space=pl.ANY),
            pl.BlockSpec((W, D), lambda i: (0, 0)),
            pl.BlockSpec((1, D), lambda i: (0, 0)),
            pl.BlockSpec((1, D), lambda i: (0, 0)),
        ],
        out_specs=pl.BlockSpec((TM_OUT, D), lambda i: (i, 0)),
        out_shape=jax.ShapeDtypeStruct((R, D), F32),
        scratch_shapes=[pltpu.VMEM((RING_SLOTS, TM_OUT, W), og.dtype),
                        pltpu.VMEM((RING_SLOTS, TM_OUT, D), x.dtype),
                        pltpu.SemaphoreType.DMA((2, RING_SLOTS))],
        compiler_params=pltpu.CompilerParams(
            dimension_semantics=("arbitrary",), vmem_limit_bytes=VMEM_LIMIT),
        name="out_ln",
    )(og, x, wo, g, b)


def _moba_proj_kernel(og_ref, xin_ref, wo_ref, lng_ref, lnb_ref, wk_ref, wvT_ref, wqT_ref, wgT_ref,
                      ck_ref, s1_ref, s2_ref, cT_ref, sT_ref,
                      x1_ref, k_ref, vT_ref, qT_ref, gT_ref, sb_ref, kmt_ref, xb_ref):
    t = pl.program_id(0)
    nb = kmt_ref.shape[0] // H
    i = lax.rem(jnp.maximum(t - 1, 0), nb)

    @pl.when(t == 0)
    def _():
        xb_ref[1] = jnp.zeros(xb_ref.shape[1:], BF16)

    @pl.when(i == 0)
    def _():
        kmt_ref[...] = jnp.zeros_like(kmt_ref)

    xb = xb_ref[lax.rem(t + 1, 2)]
    x1 = _residual_layer_norm(og_ref[0], xin_ref[0], wo_ref[...], lng_ref[...], lnb_ref[...])
    x1_ref[0] = x1
    qT = _dot_nt(wqT_ref[...], xb).reshape(H, HD_B, TM)
    k = _dot(xb, wk_ref[...])
    q_r = (qT * cT_ref[...][None] + _rot_rows(qT) * sT_ref[...][None]).reshape(H * HD_B, TM).astype(BF16)
    qT_ref[0] = q_r

    gT_ref[0] = _silu(_dot_nt(wgT_ref[...], xb)).astype(BF16)
    g = _dot(kmt_ref[...].astype(BF16), q_r).reshape(nb, H, TM)
    vT_ref[0] = _dot_nt(wvT_ref[...], xb).astype(BF16)

    parts = []
    for c in range(H * HD_B // LANES):
        kc = k[:, c * LANES:(c + 1) * LANES]
        parts.append(kc * ck_ref[...] + pltpu.roll(kc, LANES - HD_B // 2, 1) * s1_ref[...]
                     + pltpu.roll(kc, HD_B // 2, 1) * s2_ref[...])
    k_r = jnp.concatenate(parts, axis=1)
    k_ref[0] = k_r.astype(BF16)

    g_eff = [g[j] + jnp.where(j < i, 0.0, NEG) for j in range(nb)]
    rows = []
    for j in range(nb):
        rank = jnp.zeros((H, TM), F32)
        for j2 in range(nb):
            if j2 == j:
                continue
            beats = (g_eff[j2] >= g_eff[j]) if j2 < j else (g_eff[j2] > g_eff[j])
            rank = rank + jnp.where(beats, 1.0, 0.0)
        rows.append(jnp.where(rank < float(MOBA_TOPK), jnp.where(j < i, 0.0, NEG), NEG))
    sb_ref[0] = jnp.concatenate(rows, axis=0)

    k_mean = jnp.sum(k_r, axis=0, keepdims=True) * (1.0 / TM)
    head_of_lane = lax.broadcasted_iota(jnp.int32, (H, H * HD_B), 1) // HD_B
    head_of_row = lax.broadcasted_iota(jnp.int32, (H, H * HD_B), 0)
    kmt_ref[pl.ds(pl.multiple_of(i * H, H), H), :] = jnp.where(
        head_of_lane == head_of_row, jnp.broadcast_to(k_mean, (H, H * HD_B)), 0.0)

    xb_ref[lax.rem(t, 2)] = x1.astype(BF16)


def _moba_proj(og, x, wo, lng, lnb, wk, wvT, wqT, wgT, ck, s1, s2, cT, sT):
    B, S, D = x.shape
    n = S // TM
    W = H * HD_B
    tiles = B * n
    full = lambda a: pl.BlockSpec(a.shape, lambda t: (0,) * a.ndim)
    ln_tile = lambda t: jnp.minimum(t, tiles - 1)
    pr_tile = lambda t: jnp.maximum(t - 1, 0)
    ln_rows = lambda t: (ln_tile(t) // n, ln_tile(t) % n, 0)
    pr_rows = lambda t: (pr_tile(t) // n, pr_tile(t) % n, 0)
    pr_cols = lambda t: (pr_tile(t) // n, 0, pr_tile(t) % n)
    return pl.pallas_call(
        _moba_proj_kernel,
        grid=(tiles + 1,),
        in_specs=[
            pl.BlockSpec((1, TM, og.shape[2]), ln_rows),
            pl.BlockSpec((1, TM, D), ln_rows),
            full(wo), full(lng), full(lnb),
            full(wk), full(wvT), full(wqT), full(wgT),
            pl.BlockSpec((TM, LANES), lambda t: (pr_tile(t) % n, 0)),
            pl.BlockSpec((TM, LANES), lambda t: (pr_tile(t) % n, 0)),
            pl.BlockSpec((TM, LANES), lambda t: (pr_tile(t) % n, 0)),
            pl.BlockSpec((HD_B, TM), lambda t: (0, pr_tile(t) % n)),
            pl.BlockSpec((HD_B, TM), lambda t: (0, pr_tile(t) % n)),
        ],
        out_specs=[
            pl.BlockSpec((1, TM, D), ln_rows),
            pl.BlockSpec((1, TM, W), pr_rows),
            pl.BlockSpec((1, W, TM), pr_cols),
            pl.BlockSpec((1, W, TM), pr_cols),
            pl.BlockSpec((1, W, TM), pr_cols),
            pl.BlockSpec((1, n * H, TM), pr_cols),
        ],
        out_shape=[
            jax.ShapeDtypeStruct((B, S, D), F32),
            jax.ShapeDtypeStruct((B, S, W), BF16),
            jax.ShapeDtypeStruct((B, W, S), BF16),
            jax.ShapeDtypeStruct((B, W, S), BF16),
            jax.ShapeDtypeStruct((B, W, S), BF16),
            jax.ShapeDtypeStruct((B, n * H, S), F32),
        ],
        scratch_shapes=[pltpu.VMEM((n * H, W), F32), pltpu.VMEM((2, TM, D), BF16)],
        compiler_params=pltpu.CompilerParams(
            dimension_semantics=("arbitrary",), vmem_limit_bytes=VMEM_LIMIT),
        name="moba_proj",
    )(og, x, wo, lng, lnb, wk, wvT, wqT, wgT, ck, s1, s2, cT, sT)


def _moba_attn_kernel(qT_ref, k_ref, vT_ref, gT_ref, sb_ref, og_ref):
    S = k_ref.shape[1]
    first_head = 2 * PAIRS_PER_STEP * pl.program_id(1)
    zeros = jnp.zeros((HD_B, TM), BF16)

    ones = jnp.ones((BF16_ROWS, TM), BF16)

    def score_block(pl_, qi, hh, j):
        h = 2 * pl_ + hh
        q = qT_ref[0, h * HD_B:(h + 1) * HD_B, qi * TM:(qi + 1) * TM]
        q_slab = jnp.concatenate([q, zeros] if hh == 0 else [zeros, q], axis=0)
        return _dot(k_ref[0, j * TM:(j + 1) * TM, pl_ * LANES:(pl_ + 1) * LANES], q_slab)

    def bias_row(pl_, qi, hh, j):
        head = first_head + 2 * pl_ + hh
        return sb_ref[0, pl.ds(j * H + head, 1), qi * TM:(qi + 1) * TM]

    def value_block(pl_, qi, hh, j):
        h = 2 * pl_ + hh
        return jnp.concatenate([vT_ref[0, h * HD_B:(h + 1) * HD_B, j * TM:(j + 1) * TM], ones], axis=0)

    _attend_pairs(S // TM, score_block, bias_row, value_block, gT_ref, og_ref)


def _moba_attn(qT, k, vT, gT, sb):
    B, W, S = qT.shape
    hs = 2 * PAIRS_PER_STEP
    return pl.pallas_call(
        _moba_attn_kernel,
        grid=(B, H // hs),
        in_specs=[
            pl.BlockSpec((1, hs * HD_B, S), lambda b, p: (b, p, 0)),
            pl.BlockSpec((1, S, hs * HD_B), lambda b, p: (b, 0, p)),
            pl.BlockSpec((1, hs * HD_B, S), lambda b, p: (b, p, 0)),
            pl.BlockSpec((1, hs * HD_B, S), lambda b, p: (b, p, 0)),
            pl.BlockSpec((1, sb.shape[1], S), lambda b, p: (b, 0, 0)),
        ],
        out_specs=pl.BlockSpec((1, S, hs * HD_B), lambda b, p: (b, 0, p)),
        out_shape=jax.ShapeDtypeStruct((B, S, W), BF16),
        compiler_params=pltpu.CompilerParams(
            dimension_semantics=("arbitrary", "arbitrary"), vmem_limit_bytes=VMEM_LIMIT),
        name="moba_attn",
    )(qT, k, vT, gT, sb)


def _rope_tables(seq, dim):
    inv = THETA ** (-jnp.arange(0, dim, 2, dtype=F32) / dim)
    ang = jnp.arange(seq, dtype=F32)[:, None] * inv[None, :]
    ang = jnp.concatenate([ang, ang], axis=-1)
    return jnp.cos(ang), jnp.sin(ang)


def _mla_mixer(x, w_in, q_norm, kv_norm, w_uq, w_ukv, cos, sin):
    wa =w_in[:, :Q_LORA + KV_LORA].astype(BF16)
    w_kr = w_in[:, Q_LORA + KV_LORA:Q_LORA + KV_LORA + ROPE]
    w_g = w_in[:, Q_LORA + KV_LORA + ROPE:]
    wgkT = jnp.concatenate([w_g, w_kr], axis=1).astype(BF16).T
    wqT = w_uq.astype(BF16).T

    wkv = w_ukv.reshape(KV_LORA, H, NOPE + V_HD)
    wkn = wkv[..., :NOPE].reshape(KV_LORA, H * NOPE).astype(BF16)
    wvT = wkv[..., NOPE:].reshape(KV_LORA, H * V_HD).astype(BF16).T

    tk = jnp.concatenate([cos.T, sin.T], axis=0)
    tq = tk * MLA_SCALE

    qT, ks, vT, gT = _mla_proj(x, wa, q_norm[None], kv_norm[None], wqT, wkn, wvT, wgkT, tq, tk)
    return _mla_attn(qT, ks, vT, gT)


def _moba_mixer(og_prev, x_prev, w_o_prev, ln_g_prev, ln_b_prev, w_kv, w_in, cos, sin):
    S = x_prev.shape[1]
    W = H * HD_B
    half = HD_B // 2
    wk = w_kv[:, :W].astype(BF16)
    wvT = w_kv[:, W:].astype(BF16).T
    wqT = w_in[:, :W].astype(BF16).T
    wgT = w_in[:, W:].astype(BF16).T
    zeros = jnp.zeros((S, half), F32)
    ck = jnp.concatenate([cos, cos], axis=1)
    s1 = jnp.tile(jnp.concatenate([-sin[:, :half], zeros], axis=1), (1, 2))
    s2 = jnp.tile(jnp.concatenate([zeros, sin[:, half:]], axis=1), (1, 2))
    cT, sT = cos.T * MOBA_SCALE, sin.T * MOBA_SCALE
    x1, k, vT, qT, gT, sb = _moba_proj(og_prev, x_prev, w_o_prev.astype(BF16), ln_g_prev[None], ln_b_prev[None],
                                       wk, wvT, wqT, wgT, ck, s1, s2, cT, sT)
    return x1, _moba_attn(qT, k, vT, gT, sb)


def kernel(x, mla_w_in, mla_q_norm, mla_kv_norm, mla_w_uq, mla_w_ukv, mla_w_o,
           moba_w_kv, moba_w_in, moba_w_o, ln_g, ln_b):
    B, S, D = x.shape
    assert mla_w_in.shape[0] == 1 and moba_w_in.shape[0] == 1 and ln_g.shape[0] == DEPTH
    assert D == D_MODEL and S % TM_A == 0 and S % TM == 0 and (B * S) % TM_OUT == 0
    assert H % (2 * PAIRS_PER_STEP) == 0
    cos_a, sin_a = _rope_tables(S, ROPE)
    cos_b, sin_b = _rope_tables(S, HD_B)
    og_a = _mla_mixer(x, mla_w_in[0], mla_q_norm[0], mla_kv_norm[0], mla_w_uq[0], mla_w_ukv[0], cos_a, sin_a)
    x1, og_b = _moba_mixer(og_a, x, mla_w_o[0], ln_g[0], ln_b[0], moba_w_kv, moba_w_in[0], cos_b, sin_b)
    out = _out_ln(og_b.reshape(B * S, H * HD_B), x1.reshape(B * S, D), moba_w_o[0].astype(BF16),
                  ln_g[1][None], ln_b[1][None])
    return out.reshape(B, S, D)
```
